```python
import math
import jax, jax.numpy as jnp
from jax import lax
import numpy as np

D_MODEL = 1024
BATCH = 8
SEQ = 4096
DEPTH = 2
DEC_BATCH = 128
DEC_SEQ = 8
PAST_LEN = 16384
PAGE_SIZE = 128

N_MIXERS = 2
N_MLA_LAYERS = (DEPTH + 1) // 2
N_MLSTM_LAYERS = DEPTH // 2
MLA_HEADS = 16
MLA_NOPE = 64
MLA_ROPE = 32
MLA_V = 64
Q_RANK = 384
KV_RANK = 256
MLA_IN = Q_RANK + KV_RANK + MLA_ROPE
MLA_SCALE = (MLA_NOPE + MLA_ROPE) ** -0.5
ROPE_BASE = 10000.0
Q_BLOCK = 128
MLSTM_HEADS = 4
MLSTM_DK = 128
MLSTM_DV = 256
MLSTM_QK = MLSTM_HEADS * MLSTM_DK
MLSTM_HV = MLSTM_HEADS * MLSTM_DV
MLSTM_IN = 2 * MLSTM_QK + 2 * MLSTM_HV + 2 * MLSTM_HEADS
MLSTM_CHUNK = 64
GATE_CAP = 15.0
D_FF = 2816
PLE_DIM = 256
NORM_EPS = 1e-6

kernel_name = 'hybrid_mla_mlstm_macaron_step'


def rmsnorm(x, g):
    x32 = x.astype(jnp.float32)
    y = x32 * lax.rsqrt(jnp.mean(x32 * x32, axis=-1, keepdims=True) + NORM_EPS)
    return (y * g.astype(jnp.float32)).astype(x.dtype)


def swiglu(h, w_in, w_out):
    a, b = jnp.split(h @ w_in, 2, axis=-1)
    return (jax.nn.silu(a) * b) @ w_out


def rope(x, pos):
    half = x.shape[-1] // 2
    inv = ROPE_BASE ** (-jnp.arange(half, dtype=jnp.float32) / half)
    ang = pos.astype(jnp.float32)[:, None] * inv[None, :]
    cos = jnp.cos(ang)[:, None, :]
    sin = jnp.sin(ang)[:, None, :]
    x32 = x.astype(jnp.float32)
    x1, x2 = x32[..., :half], x32[..., half:]
    return jnp.concatenate([x1 * cos - x2 * sin, x1 * sin + x2 * cos], axis=-1).astype(x.dtype)


def mla_project(h, pos, w_in, g_q, g_kv, w_uq):
    z = h @ w_in
    cq = rmsnorm(z[..., :Q_RANK], g_q)
    ckv = rmsnorm(z[..., Q_RANK:Q_RANK + KV_RANK], g_kv)
    kr = rope(z[..., Q_RANK + KV_RANK:][:, :, None, :], pos)[:, :, 0, :]
    q = jnp.einsum('btr,rhd->bthd', cq, w_uq)
    q_nope = q[..., :MLA_NOPE]
    q_rope = rope(q[..., MLA_NOPE:], pos)
    return q_nope, q_rope, ckv, kr


def latent_attend(q_nope, q_rope, ckv, kr, q_pos, k_pos, w_uk):
    q_abs = jnp.einsum('...qhd,rhd->...qhr', q_nope, w_uk)
    s = jnp.einsum('...qhr,...kr->...hqk', q_abs, ckv) + jnp.einsum('...qhd,...kd->...hqk', q_rope, kr)
    s = s.astype(jnp.float32) * MLA_SCALE
    s = jnp.where(k_pos[None, :] <= q_pos[:, None], s, -jnp.inf)
    p = jax.nn.softmax(s, axis=-1).astype(ckv.dtype)
    return jnp.einsum('...hqk,...kr->...qhr', p, ckv)


def mla_attend_prompt(q_nope, q_rope, ckv, kr, w_uk):
    b, t = q_nope.shape[0], q_nope.shape[1]
    qb = math.gcd(Q_BLOCK, t)
    nb = t // qb
    pos = jnp.arange(t)

    def blocks(a):
        return jnp.moveaxis(a.reshape((b, nb, qb) + a.shape[2:]), 1, 0)

    def one(args):
        qn, qr, qp = args
        return latent_attend(qn, qr, ckv, kr, qp, pos, w_uk)

    o = lax.map(one, (blocks(q_nope), blocks(q_rope), pos.reshape(nb, qb)))
    return jnp.moveaxis(o, 0, 1).reshape(b, t, MLA_HEADS, KV_RANK)


def mla_attend_sample(q_nope, q_rope, ckv_new, kr_new, cache_ckv, cache_krope, layer, page_table, w_uk):
    past = page_table.shape[1] * cache_ckv.shape[2]
    t = q_nope.shape[1]
    q_pos = past + jnp.arange(t)
    k_pos = jnp.arange(past + t)

    def one(args):
        pt, qn, qr, cn, kn = args
        ck = cache_ckv[layer, pt].reshape(past, KV_RANK).astype(cn.dtype)
        kk = cache_krope[layer, pt].reshape(past, MLA_ROPE).astype(kn.dtype)
        ck = jnp.concatenate([ck, cn], axis=0)
        kk = jnp.concatenate([kk, kn], axis=0)
        return latent_attend(qn, qr, ck, kk, q_pos, k_pos, w_uk)

    return lax.map(one, (page_table, q_nope, q_rope, ckv_new, kr_new))


def mla_output(o_lat, w_uv, w_out):
    b, t = o_lat.shape[0], o_lat.shape[1]
    v = jnp.einsum('bthr,rhv->bthv', o_lat, w_uv).reshape(b, t, MLA_HEADS * MLA_V)
    return v @ w_out


def mlstm_chunkwise(q, k, v, ig, lf, C0, n0, m0):
    b, t = q.shape[0], q.shape[1]
    L = math.gcd(MLSTM_CHUNK, t)
    nc = t // L

    def chunks(a):
        a = jnp.swapaxes(a, 1, 2)
        a = a.reshape(a.shape[:2] + (nc, L) + a.shape[3:])
        return jnp.moveaxis(a, 2, 0)

    causal = jnp.tril(jnp.ones((L, L), dtype=bool))

    def step(carry, inp):
        C, n, m = carry
        qc, kc, vc, ic, fc = inp
        bcum = jnp.cumsum(fc, axis=-1)
        logD = ic[..., None, :] + bcum[..., :, None] - bcum[..., None, :]
        logD = jnp.where(causal, logD, -jnp.inf)
        m_t = jnp.maximum(bcum + m[..., None], jnp.max(logD, axis=-1))
        decay = jnp.exp(bcum + m[..., None] - m_t)
        S = jnp.einsum('bhtd,bhsd->bhts', qc, kc) * jnp.exp(logD - m_t[..., None])
        num = jnp.einsum('bhts,bhsv->bhtv', S, vc) + decay[..., None] * jnp.einsum('bhtd,bhdv->bhtv', qc, C)
        den = jnp.sum(S, axis=-1) + decay * jnp.einsum('bhtd,bhd->bht', qc, n)
        h = num / jnp.maximum(jnp.abs(den), jnp.exp(-m_t))[..., None]
        m_new = m_t[..., -1]
        w_s = jnp.exp(ic + bcum[..., -1:] - bcum - m_new[..., None])
        carry_decay = jnp.exp(bcum[..., -1] + m - m_new)
        C_new = carry_decay[..., None, None] * C + jnp.einsum('bhs,bhsd,bhsv->bhdv', w_s, kc, vc)
        n_new = carry_decay[..., None] * n + jnp.einsum('bhs,bhsd->bhd', w_s, kc)
        return (C_new, n_new, m_new), h

    (C, n, m), hs = lax.scan(step, (C0, n0, m0), (chunks(q), chunks(k), chunks(v), chunks(ig), chunks(lf)))
    hs = jnp.moveaxis(hs, 0, 2).reshape(b, MLSTM_HEADS, t, MLSTM_DV)
    return jnp.swapaxes(hs, 1, 2), (C, n, m)


def softcap(x):
    return GATE_CAP * jnp.tanh(x / GATE_CAP)


def mlstm_mixer(h, C0, n0, m0, w_in, b_i, b_f, g_head, w_out):
    b, t = h.shape[0], h.shape[1]
    z = h @ w_in
    cuts = [MLSTM_QK, 2 * MLSTM_QK, 2 * MLSTM_QK + MLSTM_HV, 2 * MLSTM_QK + 2 * MLSTM_HV,
            2 * MLSTM_QK + 2 * MLSTM_HV + MLSTM_HEADS]
    q, k, v, o, ig, fg = jnp.split(z.astype(jnp.float32), cuts, axis=-1)
    q = q.reshape(b, t, MLSTM_HEADS, MLSTM_DK) * (MLSTM_DK ** -0.5)
    k = k.reshape(b, t, MLSTM_HEADS, MLSTM_DK)
    v = v.reshape(b, t, MLSTM_HEADS, MLSTM_DV)
    ig = softcap(ig + b_i.astype(jnp.float32))
    lf = jax.nn.log_sigmoid(softcap(fg + b_f.astype(jnp.float32)))
    hh, state = mlstm_chunkwise(q, k, v, ig, lf, C0.astype(jnp.float32), n0.astype(jnp.float32),
                                m0.astype(jnp.float32))
    hh = hh * lax.rsqrt(jnp.mean(hh * hh, axis=-1, keepdims=True) + NORM_EPS)
    hh = hh.reshape(b, t, MLSTM_HV) * g_head.astype(jnp.float32)
    out = (jax.nn.sigmoid(o) * hh).astype(h.dtype)
    return out @ w_out, state


def setup_inputs(seed: int = 0) -> dict:
    key = jax.random.key(seed)
    keys = list(jax.random.split(key, 40))

    def nrm(shape, scale=1.0):
        return jax.random.normal(keys.pop(), shape, jnp.float32) * scale

    def gain(shape):
        return 1.0 + nrm(shape, 0.02)

    n_pages = PAST_LEN // PAGE_SIZE
    n_pool = (5 * DEC_BATCH * n_pages) // 4
    page_table = jax.random.permutation(keys.pop(), n_pool)[:DEC_BATCH * n_pages]
    page_table = page_table.reshape(DEC_BATCH, n_pages).astype(jnp.int32)
    nA, nB = N_MLA_LAYERS, N_MLSTM_LAYERS
    return {
        'x_prompt': nrm((BATCH, SEQ, D_MODEL)),
        'x_sample': nrm((DEC_BATCH, DEC_SEQ, D_MODEL)),
        'cache_ckv': nrm((nA, n_pool, PAGE_SIZE, KV_RANK)),
        'cache_krope': nrm((nA, n_pool, PAGE_SIZE, MLA_ROPE)),
        'state_C': nrm((nB, DEC_BATCH, MLSTM_HEADS, MLSTM_DK, MLSTM_DV), MLSTM_DK ** -0.5),
        'state_n': nrm((nB, DEC_BATCH, MLSTM_HEADS, MLSTM_DK)),
        'state_m': nrm((nB, DEC_BATCH, MLSTM_HEADS)),
        'page_table': page_table,
        'p_prompt': nrm((DEPTH, BATCH, SEQ, PLE_DIM)),
        'p_sample': nrm((DEPTH, DEC_BATCH, DEC_SEQ, PLE_DIM)),
        'g_ffa': gain((DEPTH, D_MODEL)),
        'w_ffa_in': nrm((DEPTH, D_MODEL, 2 * D_FF), D_MODEL ** -0.5),
        'w_ffa_out': nrm((DEPTH, D_FF, D_MODEL), D_FF ** -0.5),
        'g_mix': gain((DEPTH, D_MODEL)),
        'g_ffb': gain((DEPTH, D_MODEL)),
        'w_ffb_in': nrm((DEPTH, D_MODEL, 2 * D_FF), D_MODEL ** -0.5),
        'w_ffb_out': nrm((DEPTH, D_FF, D_MODEL), D_FF ** -0.5),
        'g_ple': gain((DEPTH, D_MODEL)),
        'w_ple_gate': nrm((DEPTH, D_MODEL, D_MODEL), D_MODEL ** -0.5),
        'w_ple_proj': nrm((DEPTH, PLE_DIM, D_MODEL), PLE_DIM ** -0.5),
        'w_mla_in': nrm((nA, D_MODEL, MLA_IN), D_MODEL ** -0.5),
        'g_q_lat': gain((nA, Q_RANK)),
        'g_kv_lat': gain((nA, KV_RANK)),
        'w_uq': nrm((nA, Q_RANK, MLA_HEADS, MLA_NOPE + MLA_ROPE), Q_RANK ** -0.5),
        'w_uk': nrm((nA, KV_RANK, MLA_HEADS, MLA_NOPE), KV_RANK ** -0.5),
        'w_uv': nrm((nA, KV_RANK, MLA_HEADS, MLA_V), KV_RANK ** -0.5),
        'w_mla_out': nrm((nA, MLA_HEADS * MLA_V, D_MODEL), (MLA_HEADS * MLA_V) ** -0.5),
        'w_mlstm_in': nrm((nB, D_MODEL, MLSTM_IN), D_MODEL ** -0.5),
        'b_igate': nrm((nB, MLSTM_HEADS), 0.1),
        'b_fgate': jnp.linspace(3.0, 6.0, MLSTM_HEADS, dtype=jnp.float32)[None, :] + nrm((nB, MLSTM_HEADS), 0.1),
        'g_mlstm_head': gain((nB, MLSTM_HV)),
        'w_mlstm_out': nrm((nB, MLSTM_HV, D_MODEL), MLSTM_HV ** -0.5),
        'g_final': gain((D_MODEL,)),
    }


def reference(x_prompt, x_sample, cache_ckv, cache_krope, state_C, state_n, state_m, page_table,
              p_prompt, p_sample, g_ffa, w_ffa_in, w_ffa_out, g_mix, g_ffb, w_ffb_in, w_ffb_out,
              g_ple, w_ple_gate, w_ple_proj, w_mla_in, g_q_lat, g_kv_lat, w_uq, w_uk, w_uv, w_mla_out,
              w_mlstm_in, b_igate, b_fgate, g_mlstm_head, w_mlstm_out, g_final):
    seq = x_prompt.shape[1]
    dec_seq = x_sample.shape[1]
    past = page_table.shape[1] * cache_ckv.shape[2]
    positions = (jnp.arange(seq), past + jnp.arange(dec_seq))
    xs = [x_prompt, x_sample]
    ps = (p_prompt, p_sample)
    ckv_out, kr_out = ([], []), ([], [])
    C_out, n_out, m_out = ([], []), ([], []), ([], [])
    for i in range(DEPTH):
        j = i // N_MIXERS
        for g in range(2):
            x = xs[g]
            pos = positions[g]
            x = x + 0.5 * swiglu(rmsnorm(x, g_ffa[i]), w_ffa_in[i], w_ffa_out[i])
            h = rmsnorm(x, g_mix[i])
            if i % N_MIXERS == 0:
                q_nope, q_rope, ckv, kr = mla_project(h, pos, w_mla_in[j], g_q_lat[j], g_kv_lat[j], w_uq[j])
                if g == 0:
                    o_lat = mla_attend_prompt(q_nope, q_rope, ckv, kr, w_uk[j])
                else:
                    o_lat = mla_attend_sample(q_nope, q_rope, ckv, kr, cache_ckv, cache_krope, j,
                                              page_table, w_uk[j])
                mix = mla_output(o_lat, w_uv[j], w_mla_out[j])
                ckv_out[g].append(ckv)
                kr_out[g].append(kr)
            else:
                if g == 0:
                    nb = x.shape[0]
                    C0 = jnp.zeros((nb, MLSTM_HEADS, MLSTM_DK, MLSTM_DV), jnp.float32)
                    n0 = jnp.zeros((nb, MLSTM_HEADS, MLSTM_DK), jnp.float32)
                    m0 = jnp.zeros((nb, MLSTM_HEADS), jnp.float32)
                else:
                    C0, n0, m0 = state_C[j], state_n[j], state_m[j]
                mix, (C, n, m) = mlstm_mixer(h, C0, n0, m0, w_mlstm_in[j], b_igate[j], b_fgate[j],
                                             g_mlstm_head[j], w_mlstm_out[j])
                C_out[g].append(C)
                n_out[g].append(n)
                m_out[g].append(m)
            x = x + mix
            x = x + 0.5 * swiglu(rmsnorm(x, g_ffb[i]), w_ffb_in[i], w_ffb_out[i])
            gate = jax.nn.sigmoid(rmsnorm(x, g_ple[i]) @ w_ple_gate[i])
            x = x + gate * (ps[g][i] @ w_ple_proj[i])
            xs[g] = x
    y_prompt = rmsnorm(xs[0], g_final)
    y_sample = rmsnorm(xs[1], g_final)
    return (y_prompt, y_sample,
            jnp.stack(ckv_out[0]), jnp.stack(kr_out[0]), jnp.stack(ckv_out[1]), jnp.stack(kr_out[1]),
            jnp.stack(C_out[0]), jnp.stack(n_out[0]), jnp.stack(m_out[0]),
            jnp.stack(C_out[1]), jnp.stack(n_out[1]), jnp.stack(m_out[1]))
```

```python
import functools
import math

import jax
import jax.numpy as jnp
import numpy as np
from jax import lax
from jax.experimental import pallas as pl
from jax.experimental.pallas import tpu as pltpu

NORM_EPS = 1e-6
ROPE_BASE = 10000.0
GATE_CAP = 15.0

LANES = 128
MXU_TILE = 256
VMEM_LIMIT = 56 * 1024 * 1024

F32 = jnp.float32
BF16 = jnp.bfloat16
HIGHEST = lax.Precision.HIGHEST


def _cparams(*sem):
    return pltpu.CompilerParams(dimension_semantics=sem, vmem_limit_bytes=VMEM_LIMIT)


def _dot(a, b):
    return jnp.dot(a, b, preferred_element_type=F32)


def _dot_nt(a, b, precision=None):
    return lax.dot_general(a, b, (((1,), (1,)), ((), ())), precision=precision,
                           preferred_element_type=F32)


def _dot_tn(a, b):
    return lax.dot_general(a, b, (((0,), (0,)), ((), ())), preferred_element_type=F32)


def _rmsnorm(x, g):
    return x * lax.rsqrt(jnp.mean(x * x, axis=-1, keepdims=True) + NORM_EPS) * g


def _full(shape):
    zeros = (0,) * len(shape)
    return pl.BlockSpec(shape, lambda *_: zeros)


def _tok_spec(bb, tt, width):
    return pl.BlockSpec((bb, tt, width), lambda b, t, *_: (b, t, 0))


def _token_tiles(B, T, rows):
    if T >= rows:
        return 1, rows
    return min(B, rows // T), T


def _ffn_body(x_ref, g_ref, wa_ref, wb_ref, wo_ref, o_ref, h_scr, acc_scr):
    f = pl.program_id(2)
    rows = h_scr.shape[0]

    @pl.when(f == 0)
    def _():
        x = x_ref[...].reshape(rows, -1)
        h_scr[...] = _rmsnorm(x, g_ref[...]).astype(BF16)
        acc_scr[...] = jnp.zeros_like(acc_scr)

    h = h_scr[...]
    a = _dot(h, wa_ref[...])
    b = _dot(h, wb_ref[...])
    act = (a * jax.nn.sigmoid(a) * b).astype(BF16)
    acc_scr[...] += _dot(act, wo_ref[...])

    @pl.when(f == pl.num_programs(2) - 1)
    def _():
        x = x_ref[...].reshape(rows, -1)
        o_ref[...] = (x + 0.5 * acc_scr[...]).reshape(o_ref.shape)


def _ffn(x, g, w_in, w_out, rows=512):
    B, T, D = x.shape
    F = w_out.shape[0]
    nf = 2 if (F // 2) % LANES == 0 else 1
    tf = F // nf
    bb, tt = _token_tiles(B, T, rows)
    return pl.pallas_call(
        _ffn_body,
        out_shape=jax.ShapeDtypeStruct(x.shape, F32),
        grid=(B // bb, T // tt, nf),
        in_specs=[
            _tok_spec(bb, tt, D),
            _full((1, D)),
            pl.BlockSpec((D, tf), lambda b, t, f: (0, f)),
            pl.BlockSpec((D, tf), lambda b, t, f: (0, f + nf)),
            pl.BlockSpec((tf, D), lambda b, t, f: (f, 0)),
        ],
        out_specs=_tok_spec(bb, tt, D),
        scratch_shapes=[pltpu.VMEM((bb * tt, D), BF16), pltpu.VMEM((bb * tt, D), F32)],
        compiler_params=_cparams("parallel", "parallel", "arbitrary"),
        name="ffn",
    )(x, g.reshape(1, D), w_in, w_in, w_out)


def _ple_body(x_ref, p_ref, g_ref, wg_ref, wp_ref, gf_ref, o_ref, *, final):
    rows = x_ref.shape[0] * x_ref.shape[1]
    x = x_ref[...].reshape(rows, -1)
    p = p_ref[...].reshape(rows, -1).astype(BF16)
    hn = _rmsnorm(x, g_ref[...]).astype(BF16)
    gate = jax.nn.sigmoid(_dot(hn, wg_ref[...]))
    y = x + gate * _dot(p, wp_ref[...])
    if final:
        y = _rmsnorm(y, gf_ref[...])
    o_ref[...] = y.reshape(o_ref.shape)


def _ple(x, p, g, w_gate, w_proj, g_final, final, rows=512):
    B, T, D = x.shape
    P = p.shape[-1]
    bb, tt = _token_tiles(B, T, rows)
    return pl.pallas_call(
        functools.partial(_ple_body, final=final),
        out_shape=jax.ShapeDtypeStruct(x.shape, F32),
        grid=(B // bb, T // tt),
        in_specs=[_tok_spec(bb, tt, D), _tok_spec(bb, tt, P), _full((1, D)),
                  _full((D, D)), _full((P, D)), _full((1, D))],
        out_specs=_tok_spec(bb, tt, D),
        compiler_params=_cparams("parallel", "parallel"),
        name="ple",
    )(x, p, g.reshape(1, D), w_gate, w_proj, g_final.reshape(1, D))


def _mla_proj_body(x_ref, gm_ref, win_ref, gq_ref, gkv_ref, wqn_ref, wqr_ref, wuk_ref, tab_ref,
                   q_ref, k_ref, ckv_ref, kr_ref, *, scale, q_rank, kv_rank, rope):
    bb, tt = x_ref.shape[0], x_ref.shape[1]
    rows = bb * tt
    heads = q_ref.shape[1]
    x = x_ref[...].reshape(rows, -1)
    h = _rmsnorm(x, gm_ref[...]).astype(BF16)
    z = _dot(h, win_ref[...])
    cq = _rmsnorm(z[:, :q_rank], gq_ref[...]).astype(BF16)
    ckv = _rmsnorm(z[:, q_rank:q_rank + kv_rank], gkv_ref[...])
    tab = jnp.broadcast_to(tab_ref[...][None], (bb, tt, LANES)).reshape(rows, LANES)
    yk = z[:, q_rank + kv_rank:] * tab
    kr = yk + pltpu.roll(yk, LANES - rope, 1)
    lane = lax.broadcasted_iota(jnp.int32, (rows, LANES), 1)
    kdup = jnp.where(lane < rope, kr, jnp.where(lane < 2 * rope, pltpu.roll(kr, rope, 1), 0.0))
    ckv_ref[...] = ckv.reshape(ckv_ref.shape)
    kr_ref[...] = kr[:, :rope].reshape(kr_ref.shape)
    k_ref[:, :, :kv_rank] = ckv.astype(BF16).reshape(bb, tt, kv_rank)
    k_ref[:, :, kv_rank:] = kdup.astype(BF16).reshape(bb, tt, LANES)

    qn = _dot(cq, wqn_ref[...]).astype(BF16)
    per = heads // (qn.shape[1] // MXU_TILE)
    for hd in range(heads):
        c0 = (hd // per) * MXU_TILE
        qa = _dot(qn[:, c0:c0 + MXU_TILE], wuk_ref[hd]) * scale
        qy = _dot(cq, wqr_ref[hd]) * tab * scale
        q_ref[:, hd, :, :kv_rank] = qa.astype(BF16).reshape(bb, tt, kv_rank)
        q_ref[:, hd, :, kv_rank:] = qy.astype(BF16).reshape(bb, tt, LANES)


def _mla_proj(x, g_mix, w, tab, rows=256):
    B, T, D = x.shape
    heads, q_rank, kv_rank, rope = w["heads"], w["q_rank"], w["kv_rank"], w["rope"]
    bb, tt = _token_tiles(B, T, rows)
    kw = kv_rank + LANES
    body = functools.partial(_mla_proj_body, scale=w["scale"], q_rank=q_rank, kv_rank=kv_rank, rope=rope)
    return pl.pallas_call(
        body,
        out_shape=(
            jax.ShapeDtypeStruct((B, heads, T, kw), BF16),
            jax.ShapeDtypeStruct((B, T, kw), BF16),
            jax.ShapeDtypeStruct((B, T, kv_rank), F32),
            jax.ShapeDtypeStruct((B, T, rope), F32),
        ),
        grid=(B // bb, T // tt),
        in_specs=[
            _tok_spec(bb, tt, D), _full((1, D)), _full(w["w_in"].shape), _full((1, q_rank)),
            _full((1, kv_rank)), _full(w["w_qn"].shape), _full(w["w_qr"].shape), _full(w["w_uk"].shape),
            pl.BlockSpec((tt, LANES), lambda b, t: (t, 0)),
        ],
        out_specs=(
            pl.BlockSpec((bb, heads, tt, kw), lambda b, t: (b, 0, t, 0)),
            _tok_spec(bb, tt, kw), _tok_spec(bb, tt, kv_rank), _tok_spec(bb, tt, rope),
        ),
        compiler_params=_cparams("parallel", "parallel"),
        name="mla_proj",
    )(x, g_mix.reshape(1, D), w["w_in"], w["g_q"].reshape(1, q_rank), w["g_kv"].reshape(1, kv_rank),
      w["w_qn"], w["w_qr"], w["w_uk"], tab)


def _prep_mla(w_mla_in, g_q, g_kv, w_uq, w_uk, w_uv, w_out):
    q_rank, heads, qd = w_uq.shape
    kv_rank, _, nope = w_uk.shape
    vdim = w_uv.shape[2]
    rope = qd - nope
    half = rope // 2
    assert kv_rank == MXU_TILE and MXU_TILE % nope == 0 and MXU_TILE % vdim == 0 and 2 * rope <= LANES
    d = w_mla_in.shape[0]
    wr = w_mla_in[:, q_rank + kv_rank:]
    w_in = jnp.concatenate(
        [w_mla_in, wr[:, half:], wr[:, :half], jnp.zeros((d, LANES - 2 * rope), F32)], axis=1).astype(BF16)
    w_qn = w_uq[:, :, :nope].reshape(q_rank, heads * nope).astype(BF16)
    x1, x2 = w_uq[:, :, nope:nope + half], w_uq[:, :, nope + half:]
    w_qr = jnp.concatenate([x1, x2, x2, x1, jnp.zeros((q_rank, heads, LANES - 2 * rope), F32)], axis=2)
    w_qr = jnp.transpose(w_qr, (1, 0, 2)).astype(BF16)
    per = MXU_TILE // nope
    slot = jax.nn.one_hot(jnp.arange(heads) % per, per, dtype=F32)
    w_ukp = jnp.einsum("rhd,hs->hsdr", w_uk, slot).reshape(heads, MXU_TILE, kv_rank).astype(BF16)
    perv = MXU_TILE // vdim
    slotv = jax.nn.one_hot(jnp.arange(heads) % perv, perv, dtype=F32)
    w_uvp = jnp.einsum("rhv,hs->hrsv", w_uv, slotv).reshape(heads, kv_rank, MXU_TILE).astype(BF16)
    return dict(heads=heads, q_rank=q_rank, kv_rank=kv_rank, rope=rope, nope=nope, vdim=vdim,
                scale=float((nope + rope) ** -0.5), w_in=w_in, g_q=g_q, g_kv=g_kv, w_qn=w_qn,
                w_qr=w_qr, w_uk=w_ukp, w_uv=w_uvp, w_out=w_out.astype(BF16))


def _rope_table(pos, rope):
    half = rope // 2
    inv = ROPE_BASE ** (-jnp.arange(half, dtype=F32) / half)
    ang = pos.astype(F32)[:, None] * inv[None, :]
    c, s = jnp.cos(ang), jnp.sin(ang)
    return jnp.concatenate([c, c, -s, s, jnp.zeros((pos.shape[0], LANES - 2 * rope), F32)], axis=1)


def _softmax_step(s, v, m_scr, l_scr, acc_scr):
    m_prev = m_scr[...]
    m_new = jnp.maximum(m_prev, jnp.max(s, axis=1, keepdims=True))
    alpha = jnp.exp(m_prev - m_new)
    p = jnp.exp(s - m_new[:, :1])
    l_scr[...] = alpha * l_scr[...] + jnp.sum(p, axis=1, keepdims=True)
    acc_scr[...] = acc_scr[...] * alpha[:, :1] + _dot(p.astype(BF16), v)
    m_scr[...] = m_new


def _softmax_init(m_scr, l_scr, acc_scr):
    m_scr[...] = jnp.full_like(m_scr, -jnp.inf)
    l_scr[...] = jnp.zeros_like(l_scr)
    acc_scr[...] = jnp.zeros_like(acc_scr)


def _attn_prompt_body(qi_ref, kj_ref, q_ref, k_ref, o_ref, m_scr, l_scr, acc_scr, *, tq, tk, kv_rank):
    s_id = pl.program_id(1)
    qi, kj = qi_ref[s_id], kj_ref[s_id]
    heads = q_ref.shape[1]
    rows = heads * tq

    @pl.when(kj == 0)
    def _():
        _softmax_init(m_scr, l_scr, acc_scr)

    q = q_ref[0].reshape(rows, -1)
    k = k_ref[0]
    s = _dot_nt(q, k)
    qpos = qi * tq + lax.broadcasted_iota(jnp.int32, (tq, tk), 0)
    kpos = kj * tk + lax.broadcasted_iota(jnp.int32, (tq, tk), 1)
    bias = jnp.where(kpos <= qpos, 0.0, -jnp.inf)
    s = (s.reshape(heads, tq, tk) + bias[None]).reshape(rows, tk)
    _softmax_step(s, k[:, :kv_rank], m_scr, l_scr, acc_scr)

    @pl.when(kj == ((qi + 1) * tq - 1) // tk)
    def _():
        o = acc_scr[...] / l_scr[...][:, :1]
        o_ref[0] = o.astype(BF16).reshape(heads, tq, kv_rank)


def _attn_prompt(q, k, kv_rank, tq=128, tk=512):
    B, heads, T, kw = q.shape
    tq, tk = min(tq, T), min(tk, T)
    pairs = [(i, j) for i in range(T // tq) for j in range(((i + 1) * tq - 1) // tk + 1)]
    qi = jnp.asarray(np.array([p[0] for p in pairs], np.int32))
    kj = jnp.asarray(np.array([p[1] for p in pairs], np.int32))
    rows = heads * tq
    grid_spec = pltpu.PrefetchScalarGridSpec(
        num_scalar_prefetch=2,
        grid=(B, len(pairs)),
        in_specs=[
            pl.BlockSpec((1, heads, tq, kw), lambda b, s, qi, kj: (b, 0, qi[s], 0)),
            pl.BlockSpec((1, tk, kw), lambda b, s, qi, kj: (b, kj[s], 0)),
        ],
        out_specs=pl.BlockSpec((1, heads, tq, kv_rank), lambda b, s, qi, kj: (b, 0, qi[s], 0)),
        scratch_shapes=[pltpu.VMEM((rows, LANES), F32), pltpu.VMEM((rows, LANES), F32),
                        pltpu.VMEM((rows, kv_rank), F32)],
    )
    return pl.pallas_call(
        functools.partial(_attn_prompt_body, tq=tq, tk=tk, kv_rank=kv_rank),
        out_shape=jax.ShapeDtypeStruct((B, heads, T, kv_rank), BF16),
        grid_spec=grid_spec,
        compiler_params=_cparams("parallel", "arbitrary"),
        name="attn_prompt",
    )(qi, kj, q, k)


def _attn_decode_body(pt_ref, q_ref, kn_ref, *refs, pages, kv_rank, rope):
    ck_refs, kr_refs = refs[:pages], refs[pages:2 * pages]
    o_ref, m_scr, l_scr, acc_scr = refs[2 * pages:]
    step = pl.program_id(1)
    heads, tn = q_ref.shape[1], q_ref.shape[2]
    rows = heads * tn

    @pl.when(step == 0)
    def _():
        _softmax_init(m_scr, l_scr, acc_scr)

    q = q_ref[0].reshape(rows, -1)
    qa = q[:, :kv_rank]
    qy = q[:, kv_rank:].astype(F32)
    qr = (qy + pltpu.roll(qy, LANES - rope, 1))[:, :rope].astype(BF16)
    for ck_ref, kr_ref in zip(ck_refs, kr_refs):
        kc = ck_ref[...].astype(BF16)
        s = _dot_nt(qa, kc) + _dot_nt(qr, kr_ref[...].astype(BF16))
        _softmax_step(s, kc, m_scr, l_scr, acc_scr)

    @pl.when(step == pl.num_programs(1) - 1)
    def _():
        kn = kn_ref[0]
        s = _dot_nt(q, kn)
        tq_ = lax.broadcasted_iota(jnp.int32, (heads, tn, tn), 1).reshape(rows, tn)
        tk_ = lax.broadcasted_iota(jnp.int32, (rows, tn), 1)
        s = jnp.where(tk_ <= tq_, s, -jnp.inf)
        _softmax_step(s, kn[:, :kv_rank], m_scr, l_scr, acc_scr)
        o = acc_scr[...] / l_scr[...][:, :1]
        o_ref[0] = o.astype(BF16).reshape(heads, tn, kv_rank)


def _attn_decode(q, k_new, cache_ckv, cache_krope, layer, page_table, kv_rank, rope, pages=8):
    B, heads, tn, kw = q.shape
    n_pages = page_table.shape[1]
    page = cache_ckv.shape[2]
    pages = math.gcd(pages, n_pages)
    rows = heads * tn

    def page_spec(j, width):
        return pl.BlockSpec((None, None, page, width),
                            lambda b, s, pt: (layer, pt[b, s * pages + j], 0, 0))

    grid_spec = pltpu.PrefetchScalarGridSpec(
        num_scalar_prefetch=1,
        grid=(B, n_pages // pages),
        in_specs=[pl.BlockSpec((1, heads, tn, kw), lambda b, s, pt: (b, 0, 0, 0)),
                  pl.BlockSpec((1, tn, kw), lambda b, s, pt: (b, 0, 0))]
                 + [page_spec(j, kv_rank) for j in range(pages)]
                 + [page_spec(j, rope) for j in range(pages)],
        out_specs=pl.BlockSpec((1, heads, tn, kv_rank), lambda b, s, pt: (b, 0, 0, 0)),
        scratch_shapes=[pltpu.VMEM((rows, LANES), F32), pltpu.VMEM((rows, LANES), F32),
                        pltpu.VMEM((rows, kv_rank), F32)],
    )
    return pl.pallas_call(
        functools.partial(_attn_decode_body, pages=pages, kv_rank=kv_rank, rope=rope),
        out_shape=jax.ShapeDtypeStruct((B, heads, tn, kv_rank), BF16),
        grid_spec=grid_spec,
        compiler_params=_cparams("parallel", "arbitrary"),
        name="attn_decode",
    )(page_table, q, k_new, *([cache_ckv] * pages), *([cache_krope] * pages))


def _mla_out_body(o_ref, x_ref, wuv_ref, wo_ref, y_ref):
    bb, heads, tt, kv_rank = o_ref.shape
    rows = bb * tt
    groups = wo_ref.shape[0] // MXU_TILE
    per = heads // groups
    vs = []
    for g in range(groups):
        acc = None
        for hd in range(g * per, (g + 1) * per):
            part = _dot(o_ref[:, hd].reshape(rows, kv_rank), wuv_ref[hd])
            acc = part if acc is None else acc + part
        vs.append(acc.astype(BF16))
    v = jnp.concatenate(vs, axis=1)
    x = x_ref[...].reshape(rows, -1)
    y_ref[...] = (x + _dot(v, wo_ref[...])).reshape(y_ref.shape)


def _mla_out(o, x, w, rows=256):
    B, T, D = x.shape
    heads, kv_rank = o.shape[1], o.shape[3]
    bb, tt = _token_tiles(B, T, rows)
    return pl.pallas_call(
        _mla_out_body,
        out_shape=jax.ShapeDtypeStruct(x.shape, F32),
        grid=(B // bb, T // tt),
        in_specs=[pl.BlockSpec((bb, heads, tt, kv_rank), lambda b, t: (b, 0, t, 0)),
                  _tok_spec(bb, tt, D), _full(w["w_uv"].shape), _full(w["w_out"].shape)],
        out_specs=_tok_spec(bb, tt, D),
        compiler_params=_cparams("parallel", "parallel"),
        name="mla_out",
    )(o, x, w["w_uv"], w["w_out"])


def _mlstm_proj_body(x_ref, g_ref, w_ref, q_ref, k_ref, v_ref, o_ref, gt_ref, *, qk, hv, qscale):
    rows = x_ref.shape[0] * x_ref.shape[1]
    x = x_ref[...].reshape(rows, -1)
    h = _rmsnorm(x, g_ref[...]).astype(BF16)
    z = _dot(h, w_ref[...])
    q_ref[...] = (z[:, :qk] * qscale).astype(BF16).reshape(q_ref.shape)
    k_ref[...] = z[:, qk:2 * qk].astype(BF16).reshape(k_ref.shape)
    v_ref[...] = z[:, 2 * qk:2 * qk + hv].astype(BF16).reshape(v_ref.shape)
    o_ref[...] = z[:, 2 * qk + hv:2 * qk + 2 * hv].reshape(o_ref.shape)
    gt_ref[...] = z[:, 2 * qk + 2 * hv:].reshape(gt_ref.shape)


def _mlstm_proj(x, g, w, qk, hv, dk, rows=512):
    B, T, D = x.shape
    bb, tt = _token_tiles(B, T, rows)
    widths = (qk, qk, hv, hv, LANES)
    dtypes = (BF16, BF16, BF16, F32, F32)
    return pl.pallas_call(
        functools.partial(_mlstm_proj_body, qk=qk, hv=hv, qscale=float(dk ** -0.5)),
        out_shape=tuple(jax.ShapeDtypeStruct((B, T, wd), dt) for wd, dt in zip(widths, dtypes)),
        grid=(B // bb, T // tt),
        in_specs=[_tok_spec(bb, tt, D), _full((1, D)), _full(w.shape)],
        out_specs=tuple(_tok_spec(bb, tt, wd) for wd in widths),
        compiler_params=_cparams("parallel", "parallel"),
        name="mlstm_proj",
    )(x, g.reshape(1, D), w)


def _mlstm_body(q_ref, k_ref, v_ref, o_ref, gt_ref, x_ref, c0_ref, n0_ref, m0_ref, bias_ref, gh_ref, wo_ref,
                y_ref, c_ref, n_ref, m_ref, c_scr, n_scr, m_scr, *, heads, dk, dv):
    ci = pl.program_id(1)
    L = q_ref.shape[1]

    @pl.when(ci == 0)
    def _():
        c_scr[...] = c0_ref[0]
        n_scr[...] = n0_ref[0]
        m_scr[...] = m0_ref[0]

    a = GATE_CAP * jnp.tanh((gt_ref[0] + bias_ref[...]) / GATE_CAP)
    lf = -(jnp.maximum(-a, 0.0) + jnp.log1p(jnp.exp(-jnp.abs(a))))
    lane = lax.broadcasted_iota(jnp.int32, (L, LANES), 1)
    xg = jnp.where(lane < heads, a, jnp.where(lane < 3 * heads, lf, 0.0))
    row = lax.broadcasted_iota(jnp.int32, (L, L), 0)
    col = lax.broadcasted_iota(jnp.int32, (L, L), 1)
    causal = col <= row
    bc = jnp.dot(causal.astype(F32), xg, precision=HIGHEST, preferred_element_type=F32)
    in_i, in_f1, in_f2 = lane < heads, (lane >= heads) & (lane < 2 * heads), (lane >= 2 * heads) & (lane < 3 * heads)
    u = jnp.where(in_f1, bc, jnp.where(in_i | in_f2, 1.0, 0.0))
    w_ = jnp.where(in_i, xg, jnp.where(in_f1, 1.0, jnp.where(in_f2, -bc, 0.0)))
    blast_row = bc[L - 1:L, :]

    outs = []
    for hd in range(heads):
        sel = (lane == hd) | (lane == heads + hd) | (lane == 2 * heads + hd)
        log_d = _dot_nt(jnp.where(sel, u, 0.0), w_, precision=HIGHEST)
        log_d = jnp.where(causal, log_d, -jnp.inf)
        b_col = jnp.sum(jnp.where(lane == heads + hd, bc, 0.0), axis=1, keepdims=True)
        i_col = jnp.sum(jnp.where(lane == hd, xg, 0.0), axis=1, keepdims=True)
        b_last = jnp.sum(jnp.where(lane[:1] == heads + hd, blast_row, 0.0), axis=1, keepdims=True)
        m_prev = m_scr[hd:hd + 1, :1]
        m_t = jnp.maximum(b_col + m_prev, jnp.max(log_d, axis=1, keepdims=True))
        decay = jnp.exp(b_col + m_prev - m_t)
        qh = q_ref[0, :, hd * dk:(hd + 1) * dk]
        kh = k_ref[0, :, hd * dk:(hd + 1) * dk]
        vh = v_ref[0, :, hd * dv:(hd + 1) * dv]
        c_prev = c_scr[hd]
        n_prev = n_scr[hd:hd + 1, :]
        sm = _dot_nt(qh, kh) * jnp.exp(log_d - m_t)
        num = _dot(sm.astype(BF16), vh) + decay * _dot(qh, c_prev.astype(BF16))
        den = jnp.sum(sm, axis=1, keepdims=True) + decay * jnp.sum(qh.astype(F32) * n_prev, axis=1, keepdims=True)
        hh = num / jnp.maximum(jnp.abs(den), jnp.exp(-m_t))
        hh = hh * lax.rsqrt(jnp.mean(hh * hh, axis=1, keepdims=True) + NORM_EPS) * gh_ref[:, hd * dv:(hd + 1) * dv]
        outs.append((jax.nn.sigmoid(o_ref[0, :, hd * dv:(hd + 1) * dv]) * hh).astype(BF16))
        m_new = m_t[L - 1:L, :]
        wgt = jnp.exp(i_col + b_last - b_col - m_new)
        cdec = jnp.exp(b_last + m_prev - m_new)
        kw_ = kh.astype(F32) * wgt
        c_scr[hd] = cdec * c_prev + _dot_tn(kw_.astype(BF16), vh)
        n_scr[hd:hd + 1, :] = cdec * n_prev + jnp.sum(kw_, axis=0, keepdims=True)
        m_scr[hd:hd + 1, :] = jnp.broadcast_to(m_new, (1, LANES))

    out = jnp.concatenate(outs, axis=1)
    y_ref[0] = x_ref[0] + _dot(out, wo_ref[...])

    @pl.when(ci == pl.num_programs(1) - 1)
    def _():
        c_ref[0] = c_scr[...]
        n_ref[0] = n_scr[...]
        m_ref[0] = m_scr[...]


def _mlstm(q, k, v, o, gt, x, c0, n0, m0, bias, g_head, w_out, chunk=256):
    B, T, D = x.shape
    heads, dk, dv = c0.shape[1], c0.shape[2], c0.shape[3]
    L = min(chunk, T)
    tok = lambda wd: pl.BlockSpec((1, L, wd), lambda b, c: (b, c, 0))
    st3 = pl.BlockSpec((1, heads, LANES), lambda b, c: (b, 0, 0))
    st4 = pl.BlockSpec((1, heads, dk, dv), lambda b, c: (b, 0, 0, 0))
    return pl.pallas_call(
        functools.partial(_mlstm_body, heads=heads, dk=dk, dv=dv),
        out_shape=(jax.ShapeDtypeStruct(x.shape, F32),
                   jax.ShapeDtypeStruct(c0.shape, F32),
                   jax.ShapeDtypeStruct((B, heads, dk), F32),
                   jax.ShapeDtypeStruct((B, heads, LANES), F32)),
        grid=(B, T // L),
        in_specs=[tok(heads * dk), tok(heads * dk), tok(heads * dv), tok(heads * dv), tok(LANES), tok(D),
                  st4, st3, st3, _full((1, LANES)), _full((1, heads * dv)), _full(w_out.shape)],
        out_specs=(tok(D), st4, st3, st3),
        scratch_shapes=[pltpu.VMEM((heads, dk, dv), F32), pltpu.VMEM((heads, dk), F32),
                        pltpu.VMEM((heads, LANES), F32)],
        compiler_params=_cparams("parallel", "arbitrary"),
        name="mlstm",
    )(q, k, v, o, gt, x, c0, n0, m0, bias, g_head.reshape(1, heads * dv), w_out)


def kernel(x_prompt, x_sample, cache_ckv, cache_krope, state_C, state_n, state_m, page_table, p_prompt, p_sample, g_ffa, w_ffa_in, w_ffa_out, g_mix, g_ffb, w_ffb_in, w_ffb_out, g_ple, w_ple_gate, w_ple_proj, w_mla_in, g_q_lat, g_kv_lat, w_uq, w_uk, w_uv, w_mla_out, w_mlstm_in, b_igate, b_fgate, g_mlstm_head, w_mlstm_out, g_final):
    depth = g_ffa.shape[0]
    seq, dec_seq = x_prompt.shape[1], x_sample.shape[1]
    past = page_table.shape[1] * cache_ckv.shape[2]
    positions = (jnp.arange(seq), past + jnp.arange(dec_seq))
    xs = [x_prompt, x_sample]
    ps = (p_prompt, p_sample)
    heads_m, dk, dv = state_C.shape[2], state_C.shape[3], state_C.shape[4]
    assert dk == LANES and 3 * heads_m <= LANES
    qk, hv = heads_m * dk, heads_m * dv
    outs = {name: ([], []) for name in ("ckv", "kr", "C", "n", "m")}

    for i in range(depth):
        j = i // 2
        bf = lambda a: a.astype(BF16)
        w_ffa = (bf(w_ffa_in[i]), bf(w_ffa_out[i]))
        w_ffb = (bf(w_ffb_in[i]), bf(w_ffb_out[i]))
        w_gate, w_proj = bf(w_ple_gate[i]), bf(w_ple_proj[i])
        if i % 2 == 0:
            mla = _prep_mla(w_mla_in[j], g_q_lat[j], g_kv_lat[j], w_uq[j], w_uk[j], w_uv[j], w_mla_out[j])
        else:
            wi = w_mlstm_in[j]
            gates = wi[:, 2 * qk + 2 * hv:]
            ig, fg = gates[:, :heads_m], gates[:, heads_m:]
            pad = jnp.zeros((wi.shape[0], LANES - 3 * heads_m), F32)
            w_ml = bf(jnp.concatenate([wi[:, :2 * qk + 2 * hv], ig, fg, fg, pad], axis=1))
            bias = jnp.concatenate([b_igate[j], b_fgate[j], b_fgate[j],
                                    jnp.zeros((LANES - 3 * heads_m,), F32)]).reshape(1, LANES)
            w_mo = bf(w_mlstm_out[j])
        for g in range(2):
            x = xs[g]
            nb = x.shape[0]
            x = _ffn(x, g_ffa[i], *w_ffa)
            if i % 2 == 0:
                tab = _rope_table(positions[g], mla["rope"])
                q, k, ckv, kr = _mla_proj(x, g_mix[i], mla, tab)
                if g == 0:
                    o = _attn_prompt(q, k, mla["kv_rank"])
                else:
                    o = _attn_decode(q, k, cache_ckv, cache_krope, j, page_table, mla["kv_rank"], mla["rope"])
                x = _mla_out(o, x, mla)
                outs["ckv"][g].append(ckv)
                outs["kr"][g].append(kr)
            else:
                if g == 0:
                    c0 = jnp.zeros((nb, heads_m, dk, dv), F32)
                    n0 = jnp.zeros((nb, heads_m, dk), F32)
                    m0 = jnp.zeros((nb, heads_m), F32)
                else:
                    c0, n0, m0 = state_C[j], state_n[j], state_m[j]
                m0 = jnp.broadcast_to(m0[:, :, None], (nb, heads_m, LANES))
                qm, km, vm, om, gt = _mlstm_proj(x, g_mix[i], w_ml, qk, hv, dk)
                x, c_new, n_new, m_new = _mlstm(qm, km, vm, om, gt, x, c0, n0, m0, bias, g_mlstm_head[j], w_mo)
                outs["C"][g].append(c_new)
                outs["n"][g].append(n_new)
                outs["m"][g].append(m_new[:, :, 0])
            x = _ffn(x, g_ffb[i], *w_ffb)
            x = _ple(x, ps[g][i], g_ple[i], w_gate, w_proj, g_final, final=(i == depth - 1))
            xs[g] = x

    st = lambda name, g: jnp.stack(outs[name][g])
    return (xs[0], xs[1], st("ckv", 0), st("kr", 0), st("ckv", 1), st("kr", 1),
            st("C", 0), st("n", 0), st("m", 0), st("C", 1), st("n", 1), st("m", 1))
```

```python
import functools
import math

import jax
import jax.numpy as jnp
import numpy as np
from jax import lax
from jax.experimental import pallas as pl
from jax.experimental.pallas import tpu as pltpu

NORM_EPS = 1e-6
ROPE_BASE = 10000.0
GATE_CAP = 15.0

LANES = 128
MXU_TILE = 256
VMEM_LIMIT = 56 * 1024 * 1024

F32 = jnp.float32
BF16 = jnp.bfloat16
HIGHEST = lax.Precision.HIGHEST


def _cparams(*sem):
    return pltpu.CompilerParams(dimension_semantics=sem, vmem_limit_bytes=VMEM_LIMIT)


def _dot(a, b):
    return jnp.dot(a, b, preferred_element_type=F32)


def _dot_nt(a, b, precision=None):
    return lax.dot_general(a, b, (((1,), (1,)), ((), ())), precision=precision,
                           preferred_element_type=F32)


def _dot_tn(a, b):
    return lax.dot_general(a, b, (((0,), (0,)), ((), ())), preferred_element_type=F32)


def _rmsnorm(x, g):
    return x * lax.rsqrt(jnp.mean(x * x, axis=-1, keepdims=True) + NORM_EPS) * g


def _full(shape):
    zeros = (0,) * len(shape)
    return pl.BlockSpec(shape, lambda *_: zeros)


def _tok_spec(bb, tt, width):
    return pl.BlockSpec((bb, tt, width), lambda b, t, *_: (b, t, 0))


def _token_tiles(B, T, rows):
    if T >= rows:
        return 1, rows
    return min(B, rows // T), T


def _ffn_body(x_ref, g_ref, win_ref, wo_ref, o_ref, *, sub):
    bb, tt, d = x_ref.shape
    f = wo_ref.shape[0]
    nb = sub // tt if bb > 1 else 1
    for r in range(bb * tt // sub):
        if bb == 1:
            x = x_ref[0, r * sub:(r + 1) * sub, :]
        else:
            x = x_ref[r * nb:(r + 1) * nb].reshape(sub, d)
        h = _rmsnorm(x, g_ref[...]).astype(BF16)
        a = _dot(h, win_ref[:, :f])
        b = _dot(h, win_ref[:, f:])
        act = (a * jax.nn.sigmoid(a) * b).astype(BF16)
        y = x + 0.5 * _dot(act, wo_ref[...])
        if bb == 1:
            o_ref[0, r * sub:(r + 1) * sub, :] = y
        else:
            o_ref[r * nb:(r + 1) * nb] = y.reshape(nb, tt, d)


def _ffn(x, g, w_in, w_out, rows=512, sub=256):
    B, T, D = x.shape
    F = w_out.shape[0]
    bb, tt = _token_tiles(B, T, rows)
    sub = min(sub, bb * tt)
    resident = pl.Buffered(1)
    return pl.pallas_call(
        functools.partial(_ffn_body, sub=sub),
        out_shape=jax.ShapeDtypeStruct(x.shape, F32),
        grid=(B // bb, T // tt),
        in_specs=[
            _tok_spec(bb, tt, D),
            _full((1, D)),
            pl.BlockSpec((D, 2 * F), lambda b, t: (0, 0), pipeline_mode=resident),
            pl.BlockSpec((F, D), lambda b, t: (0, 0), pipeline_mode=resident),
        ],
        out_specs=_tok_spec(bb, tt, D),
        compiler_params=_cparams("parallel", "parallel"),
        name="ffn",
    )(x, g.reshape(1, D), w_in, w_out)


def _ple_body(x_ref, p_ref, g_ref, wg_ref, wp_ref, gf_ref, o_ref, *, final):
    rows = x_ref.shape[0] * x_ref.shape[1]
    x = x_ref[...].reshape(rows, -1)
    p = p_ref[...].reshape(rows, -1).astype(BF16)
    hn = _rmsnorm(x, g_ref[...]).astype(BF16)
    gate = jax.nn.sigmoid(_dot(hn, wg_ref[...]))
    y = x + gate * _dot(p, wp_ref[...])
    if final:
        y = _rmsnorm(y, gf_ref[...])
    o_ref[...] = y.reshape(o_ref.shape)


def _ple(x, p, g, w_gate, w_proj, g_final, final, rows=512):
    B, T, D = x.shape
    P = p.shape[-1]
    bb, tt = _token_tiles(B, T, rows)
    return pl.pallas_call(
        functools.partial(_ple_body, final=final),
        out_shape=jax.ShapeDtypeStruct(x.shape, F32),
        grid=(B // bb, T // tt),
        in_specs=[_tok_spec(bb, tt, D), _tok_spec(bb, tt, P), _full((1, D)),
                  _full((D, D)), _full((P, D)), _full((1, D))],
        out_specs=_tok_spec(bb, tt, D),
        compiler_params=_cparams("parallel", "parallel"),
        name="ple",
    )(x, p, g.reshape(1, D), w_gate, w_proj, g_final.reshape(1, D))


def _mla_proj_body(x_ref, gm_ref, win_ref, gq_ref, gkv_ref, wqn_ref, wqr_ref, wuk_ref, tab_ref,
                   q_ref, k_ref, ckv_ref, kr_ref, *, scale, q_rank, kv_rank, rope):
    bb, tt = x_ref.shape[0], x_ref.shape[1]
    rows = bb * tt
    heads = q_ref.shape[1]
    x = x_ref[...].reshape(rows, -1)
    h = _rmsnorm(x, gm_ref[...]).astype(BF16)
    z = _dot(h, win_ref[...])
    cq = _rmsnorm(z[:, :q_rank], gq_ref[...]).astype(BF16)
    ckv = _rmsnorm(z[:, q_rank:q_rank + kv_rank], gkv_ref[...])
    tab = jnp.broadcast_to(tab_ref[...][None], (bb, tt, LANES)).reshape(rows, LANES)
    yk = z[:, q_rank + kv_rank:] * tab
    kr = yk + pltpu.roll(yk, LANES - rope, 1)
    lane = lax.broadcasted_iota(jnp.int32, (rows, LANES), 1)
    kdup = jnp.where(lane < rope, kr, jnp.where(lane < 2 * rope, pltpu.roll(kr, rope, 1), 0.0))
    ckv_ref[...] = ckv.reshape(ckv_ref.shape)
    kr_ref[...] = kr[:, :rope].reshape(kr_ref.shape)
    k_ref[:, :, :kv_rank] = ckv.astype(BF16).reshape(bb, tt, kv_rank)
    k_ref[:, :, kv_rank:] = kdup.astype(BF16).reshape(bb, tt, LANES)

    qn = _dot(cq, wqn_ref[...]).astype(BF16)
    per = heads // (qn.shape[1] // MXU_TILE)
    for hd in range(heads):
        c0 = (hd // per) * MXU_TILE
        qa = _dot(qn[:, c0:c0 + MXU_TILE], wuk_ref[hd]) * scale
        qy = _dot(cq, wqr_ref[hd]) * tab * scale
        q_ref[:, hd, :, :kv_rank] = qa.astype(BF16).reshape(bb, tt, kv_rank)
        q_ref[:, hd, :, kv_rank:] = qy.astype(BF16).reshape(bb, tt, LANES)


def _mla_proj(x, g_mix, w, tab, rows=256):
    B, T, D = x.shape
    heads, q_rank, kv_rank, rope = w["heads"], w["q_rank"], w["kv_rank"], w["rope"]
    bb, tt = _token_tiles(B, T, rows)
    kw = kv_rank + LANES
    body = functools.partial(_mla_proj_body, scale=w["scale"], q_rank=q_rank, kv_rank=kv_rank, rope=rope)
    return pl.pallas_call(
        body,
        out_shape=(
            jax.ShapeDtypeStruct((B, heads, T, kw), BF16),
            jax.ShapeDtypeStruct((B, T, kw), BF16),
            jax.ShapeDtypeStruct((B, T, kv_rank), F32),
            jax.ShapeDtypeStruct((B, T, rope), F32),
        ),
        grid=(B // bb, T // tt),
        in_specs=[
            _tok_spec(bb, tt, D), _full((1, D)), _full(w["w_in"].shape), _full((1, q_rank)),
            _full((1, kv_rank)), _full(w["w_qn"].shape), _full(w["w_qr"].shape), _full(w["w_uk"].shape),
            pl.BlockSpec((tt, LANES), lambda b, t: (t, 0)),
        ],
        out_specs=(
            pl.BlockSpec((bb, heads, tt, kw), lambda b, t: (b, 0, t, 0)),
            _tok_spec(bb, tt, kw), _tok_spec(bb, tt, kv_rank), _tok_spec(bb, tt, rope),
        ),
        compiler_params=_cparams("parallel", "parallel"),
        name="mla_proj",
    )(x, g_mix.reshape(1, D), w["w_in"], w["g_q"].reshape(1, q_rank), w["g_kv"].reshape(1, kv_rank),
      w["w_qn"], w["w_qr"], w["w_uk"], tab)


def _prep_mla(w_mla_in, g_q, g_kv, w_uq, w_uk, w_uv, w_out):
    q_rank, heads, qd = w_uq.shape
    kv_rank, _, nope = w_uk.shape
    vdim = w_uv.shape[2]
    rope = qd - nope
    half = rope // 2
    assert kv_rank == MXU_TILE and MXU_TILE % nope == 0 and MXU_TILE % vdim == 0 and 2 * rope <= LANES
    d = w_mla_in.shape[0]
    wr = w_mla_in[:, q_rank + kv_rank:]
    w_in = jnp.concatenate(
        [w_mla_in, wr[:, half:], wr[:, :half], jnp.zeros((d, LANES - 2 * rope), F32)], axis=1).astype(BF16)
    w_qn = w_uq[:, :, :nope].reshape(q_rank, heads * nope).astype(BF16)
    x1, x2 = w_uq[:, :, nope:nope + half], w_uq[:, :, nope + half:]
    w_qr = jnp.concatenate([x1, x2, x2, x1, jnp.zeros((q_rank, heads, LANES - 2 * rope), F32)], axis=2)
    w_qr = jnp.transpose(w_qr, (1, 0, 2)).astype(BF16)
    per = MXU_TILE // nope
    slot = jax.nn.one_hot(jnp.arange(heads) % per, per, dtype=F32)
    w_ukp = jnp.einsum("rhd,hs->hsdr", w_uk, slot).reshape(heads, MXU_TILE, kv_rank).astype(BF16)
    perv = MXU_TILE // vdim
    slotv = jax.nn.one_hot(jnp.arange(heads) % perv, perv, dtype=F32)
    w_uvp = jnp.einsum("rhv,hs->hrsv", w_uv, slotv).reshape(heads, kv_rank, MXU_TILE).astype(BF16)
    return dict(heads=heads, q_rank=q_rank, kv_rank=kv_rank, rope=rope, nope=nope, vdim=vdim,
                scale=float((nope + rope) ** -0.5 * math.log2(math.e)), w_in=w_in, g_q=g_q, g_kv=g_kv, w_qn=w_qn,
                w_qr=w_qr, w_uk=w_ukp, w_uv=w_uvp, w_out=w_out.astype(BF16))


def _rope_table(pos, rope):
    half = rope // 2
    inv = ROPE_BASE ** (-jnp.arange(half, dtype=F32) / half)
    ang = pos.astype(F32)[:, None] * inv[None, :]
    c, s = jnp.cos(ang), jnp.sin(ang)
    return jnp.concatenate([c, c, -s, s, jnp.zeros((pos.shape[0], LANES - 2 * rope), F32)], axis=1)


def _lane_tile(stat, width):
    if width % LANES:
        return stat[:, :1]
    return stat if width == LANES else jnp.tile(stat, (1, width // LANES))


def _softmax_step(s, v, m_scr, l_scr, acc_scr):
    m_prev = m_scr[...]
    m_new = jnp.maximum(m_prev, jnp.max(s, axis=1, keepdims=True))
    alpha = jnp.exp2(m_prev - m_new)
    p = jnp.exp2(s - _lane_tile(m_new, s.shape[1]))
    l_scr[...] = alpha * l_scr[...] + jnp.sum(p, axis=1, keepdims=True)
    acc_scr[...] = acc_scr[...] * _lane_tile(alpha, v.shape[1]) + _dot(p.astype(BF16), v)
    m_scr[...] = m_new


def _softmax_result(l_scr, acc_scr):
    return acc_scr[...] / _lane_tile(l_scr[...], acc_scr.shape[-1])


def _softmax_init(m_scr, l_scr, acc_scr):
    m_scr[...] = jnp.full_like(m_scr, -jnp.inf)
    l_scr[...] = jnp.zeros_like(l_scr)
    acc_scr[...] = jnp.zeros_like(acc_scr)


def _attn_prompt_body(qi_ref, kj_ref, q_ref, k_ref, o_ref, m_scr, l_scr, acc_scr, *, tq, tk, kv_rank):
    s_id = pl.program_id(1)
    qi, kj = qi_ref[s_id], kj_ref[s_id]
    heads = q_ref.shape[1]
    groups, rows = m_scr.shape[0], m_scr.shape[1]
    hb = heads // groups

    @pl.when(kj == 0)
    def _():
        _softmax_init(m_scr, l_scr, acc_scr)

    k = k_ref[0]
    v = k[:, :kv_rank]
    qpos = qi * tq + lax.broadcasted_iota(jnp.int32, (tq, tk), 0)
    kpos = kj * tk + lax.broadcasted_iota(jnp.int32, (tq, tk), 1)
    bias = jnp.where(kpos <= qpos, 0.0, -jnp.inf)
    def scores(g):
        q = q_ref[0, g * hb:(g + 1) * hb].reshape(rows, -1)
        return (_dot_nt(q, k).reshape(hb, tq, tk) + bias[None]).reshape(rows, tk)

    s_next = scores(0)
    for g in range(groups):
        s = s_next
        if g + 1 < groups:
            s_next = scores(g + 1)
        _softmax_step(s, v, m_scr.at[g], l_scr.at[g], acc_scr.at[g])

    @pl.when(kj == ((qi + 1) * tq - 1) // tk)
    def _():
        for g in range(groups):
            o = _softmax_result(l_scr.at[g], acc_scr.at[g]).astype(BF16)
            o_ref[0, g * hb:(g + 1) * hb] = o.reshape(hb, tq, kv_rank)


def _attn_prompt(q, k, kv_rank, tq=128, tk=512, hb=2):
    B, heads, T, kw = q.shape
    tq, tk = min(tq, T), min(tk, T)
    groups, rows = heads // hb, hb * tq
    pairs = [(i, j) for i in range(T // tq) for j in range(((i + 1) * tq - 1) // tk + 1)]
    qi = jnp.asarray(np.array([p[0] for p in pairs], np.int32))
    kj = jnp.asarray(np.array([p[1] for p in pairs], np.int32))
    grid_spec = pltpu.PrefetchScalarGridSpec(
        num_scalar_prefetch=2,
        grid=(B, len(pairs)),
        in_specs=[
            pl.BlockSpec((1, heads, tq, kw), lambda b, s, qi, kj: (b, 0, qi[s], 0)),
            pl.BlockSpec((1, tk, kw), lambda b, s, qi, kj: (b, kj[s], 0)),
        ],
        out_specs=pl.BlockSpec((1, heads, tq, kv_rank), lambda b, s, qi, kj: (b, 0, qi[s], 0)),
        scratch_shapes=[pltpu.VMEM((groups, rows, LANES), F32), pltpu.VMEM((groups, rows, LANES), F32),
                        pltpu.VMEM((groups, rows, kv_rank), F32)],
    )
    return pl.pallas_call(
        functools.partial(_attn_prompt_body, tq=tq, tk=tk, kv_rank=kv_rank),
        out_shape=jax.ShapeDtypeStruct((B, heads, T, kv_rank), BF16),
        grid_spec=grid_spec,
        compiler_params=_cparams("parallel", "arbitrary"),
        name="attn_prompt",
    )(qi, kj, q, k)


def _attn_decode_body(pt_ref, q_ref, kn_ref, *refs, pages, group, kv_rank, rope):
    ck_refs, kr_refs = refs[:pages], refs[pages:2 * pages]
    o_ref, m_scr, l_scr, acc_scr = refs[2 * pages:]
    step = pl.program_id(1)
    heads, tn = q_ref.shape[1], q_ref.shape[2]
    rows = heads * tn

    @pl.when(step == 0)
    def _():
        _softmax_init(m_scr, l_scr, acc_scr)

    q = q_ref[0].reshape(rows, -1)
    qa = q[:, :kv_rank]
    qy = q[:, kv_rank:].astype(F32)
    qr = (qy + pltpu.roll(qy, LANES - rope, 1))[:, :rope].astype(BF16)

    scores = []
    for g0 in range(0, pages, group):
        kc = jnp.concatenate([r[...].astype(BF16) for r in ck_refs[g0:g0 + group]], axis=0)
        krt = jnp.concatenate([r[...].astype(BF16) for r in kr_refs[g0:g0 + group]], axis=1)
        scores.append((_dot_nt(qa, kc) + _dot(qr, krt), kc))
    parts = []
    for s, kc in scores:
        m_g = jnp.broadcast_to(jnp.max(s, axis=1, keepdims=True), (rows, LANES))
        p = jnp.exp2(s - _lane_tile(m_g, s.shape[1]))
        parts.append((m_g, jnp.sum(p, axis=1, keepdims=True), _dot(p.astype(BF16), kc)))
    m_prev = m_scr[...]
    m_new = functools.reduce(jnp.maximum, [m for m, _, _ in parts], m_prev)
    a_prev = jnp.exp2(m_prev - m_new)
    l_new = a_prev * l_scr[...]
    acc = acc_scr[...] * _lane_tile(a_prev, kv_rank)
    for m_g, l_g, pv in parts:
        a_g = jnp.exp2(m_g - m_new)
        l_new = l_new + a_g * l_g
        acc = acc + pv * _lane_tile(a_g, kv_rank)
    m_scr[...] = m_new
    l_scr[...] = l_new
    acc_scr[...] = acc

    @pl.when(step == pl.num_programs(1) - 1)
    def _():
        kn = kn_ref[0]
        s = _dot_nt(q, kn)
        tq_ = lax.broadcasted_iota(jnp.int32, (heads, tn, tn), 1).reshape(rows, tn)
        tk_ = lax.broadcasted_iota(jnp.int32, (rows, tn), 1)
        s = jnp.where(tk_ <= tq_, s, -jnp.inf)
        _softmax_step(s, kn[:, :kv_rank], m_scr, l_scr, acc_scr)
        o_ref[0] = _softmax_result(l_scr, acc_scr).astype(BF16).reshape(heads, tn, kv_rank)


def _attn_decode(q, k_new, cache_ckv, cache_krope_t, layer, page_table, kv_rank, rope, pages=16, group=4):
    B, heads, tn, kw = q.shape
    n_pages = page_table.shape[1]
    page = cache_ckv.shape[2]
    pages = math.gcd(pages, n_pages)
    group = math.gcd(group, pages)
    rows = heads * tn

    def page_spec(j, shape):
        return pl.BlockSpec((None, None) + shape, lambda b, s, pt: (layer, pt[b, s * pages + j], 0, 0))

    grid_spec = pltpu.PrefetchScalarGridSpec(
        num_scalar_prefetch=1,
        grid=(B, n_pages // pages),
        in_specs=[pl.BlockSpec((1, heads, tn, kw), lambda b, s, pt: (b, 0, 0, 0)),
                  pl.BlockSpec((1, tn, kw), lambda b, s, pt: (b, 0, 0))]
                 + [page_spec(j, (page, kv_rank)) for j in range(pages)]
                 + [page_spec(j, (rope, page)) for j in range(pages)],
        out_specs=pl.BlockSpec((1, heads, tn, kv_rank), lambda b, s, pt: (b, 0, 0, 0)),
        scratch_shapes=[pltpu.VMEM((rows, LANES), F32), pltpu.VMEM((rows, LANES), F32),
                        pltpu.VMEM((rows, kv_rank), F32)],
    )
    return pl.pallas_call(
        functools.partial(_attn_decode_body, pages=pages, group=group, kv_rank=kv_rank, rope=rope),
        out_shape=jax.ShapeDtypeStruct((B, heads, tn, kv_rank), BF16),
        grid_spec=grid_spec,
        compiler_params=_cparams("parallel", "arbitrary"),
        name="attn_decode",
    )(page_table, q, k_new, *([cache_ckv] * pages), *([cache_krope_t] * pages))


def _mla_out_body(o_ref, x_ref, wuv_ref, wo_ref, y_ref):
    bb, heads, tt, kv_rank = o_ref.shape
    rows = bb * tt
    groups = wo_ref.shape[0] // MXU_TILE
    per = heads // groups
    vs = []
    for g in range(groups):
        acc = None
        for hd in range(g * per, (g + 1) * per):
            part = _dot(o_ref[:, hd].reshape(rows, kv_rank), wuv_ref[hd])
            acc = part if acc is None else acc + part
        vs.append(acc.astype(BF16))
    v = jnp.concatenate(vs, axis=1)
    x = x_ref[...].reshape(rows, -1)
    y_ref[...] = (x + _dot(v, wo_ref[...])).reshape(y_ref.shape)


def _mla_out(o, x, w, rows=256):
    B, T, D = x.shape
    heads, kv_rank = o.shape[1], o.shape[3]
    bb, tt = _token_tiles(B, T, rows)
    return pl.pallas_call(
        _mla_out_body,
        out_shape=jax.ShapeDtypeStruct(x.shape, F32),
        grid=(B // bb, T // tt),
        in_specs=[pl.BlockSpec((bb, heads, tt, kv_rank), lambda b, t: (b, 0, t, 0)),
                  _tok_spec(bb, tt, D), _full(w["w_uv"].shape), _full(w["w_out"].shape)],
        out_specs=_tok_spec(bb, tt, D),
        compiler_params=_cparams("parallel", "parallel"),
        name="mla_out",
    )(o, x, w["w_uv"], w["w_out"])


def _mlstm_proj_body(x_ref, g_ref, w_ref, q_ref, k_ref, v_ref, o_ref, gt_ref, *, qk, hv, qscale):
    rows = x_ref.shape[0] * x_ref.shape[1]
    x = x_ref[...].reshape(rows, -1)
    h = _rmsnorm(x, g_ref[...]).astype(BF16)
    z = _dot(h, w_ref[...])
    q_ref[...] = (z[:, :qk] * qscale).astype(BF16).reshape(q_ref.shape)
    k_ref[...] = z[:, qk:2 * qk].astype(BF16).reshape(k_ref.shape)
    v_ref[...] = z[:, 2 * qk:2 * qk + hv].astype(BF16).reshape(v_ref.shape)
    o_ref[...] = z[:, 2 * qk + hv:2 * qk + 2 * hv].reshape(o_ref.shape)
    gt_ref[...] = z[:, 2 * qk + 2 * hv:].reshape(gt_ref.shape)


def _mlstm_proj(x, g, w, qk, hv, dk, rows=512):
    B, T, D = x.shape
    bb, tt = _token_tiles(B, T, rows)
    widths = (qk, qk, hv, hv, LANES)
    dtypes = (BF16, BF16, BF16, F32, F32)
    return pl.pallas_call(
        functools.partial(_mlstm_proj_body, qk=qk, hv=hv, qscale=float(dk ** -0.5)),
        out_shape=tuple(jax.ShapeDtypeStruct((B, T, wd), dt) for wd, dt in zip(widths, dtypes)),
        grid=(B // bb, T // tt),
        in_specs=[_tok_spec(bb, tt, D), _full((1, D)), _full(w.shape)],
        out_specs=tuple(_tok_spec(bb, tt, wd) for wd in widths),
        compiler_params=_cparams("parallel", "parallel"),
        name="mlstm_proj",
    )(x, g.reshape(1, D), w)


def _mlstm_body(q_ref, k_ref, v_ref, o_ref, gt_ref, x_ref, c0_ref, n0_ref, m0_ref, bias_ref, gh_ref, wo_ref,
                y_ref, c_ref, n_ref, m_ref, c_scr, n_scr, m_scr, *, heads, dk, dv):
    ci = pl.program_id(1)
    L = q_ref.shape[1]

    @pl.when(ci == 0)
    def _():
        c_scr[...] = c0_ref[0]
        n_scr[...] = n0_ref[0]
        m_scr[...] = m0_ref[0]

    a = GATE_CAP * jnp.tanh((gt_ref[0] + bias_ref[...]) / GATE_CAP)
    lf = -(jnp.maximum(-a, 0.0) + jnp.log1p(jnp.exp(-jnp.abs(a))))
    lane = lax.broadcasted_iota(jnp.int32, (L, LANES), 1)
    xg = jnp.where(lane < heads, a, jnp.where(lane < 2 * heads, lf, 0.0))
    row = lax.broadcasted_iota(jnp.int32, (L, L), 0)
    col = lax.broadcasted_iota(jnp.int32, (L, L), 1)
    causal = col <= row
    bc = jnp.dot(causal.astype(F32), xg, precision=HIGHEST, preferred_element_type=F32)
    blast_row = bc[L - 1:L, :]
    xg_t, bc_t = xg.T, bc.T

    pre = []
    for hd in range(heads):
        qh = q_ref[0, :, hd * dk:(hd + 1) * dk]
        kh = k_ref[0, :, hd * dk:(hd + 1) * dk]
        pre.append((_dot_nt(qh, kh), _dot(qh, c_scr[hd].astype(BF16))))

    outs = []
    for hd in range(heads):
        qk_, qc = pre[hd]
        b_col = jnp.sum(jnp.where(lane == heads + hd, bc, 0.0), axis=1, keepdims=True)
        ib_row = xg_t[hd:hd + 1, :] - bc_t[heads + hd:heads + hd + 1, :]
        log_d = jnp.where(causal, b_col + ib_row, -jnp.inf)
        i_col = jnp.sum(jnp.where(lane == hd, xg, 0.0), axis=1, keepdims=True)
        b_last = jnp.sum(jnp.where(lane[:1] == heads + hd, blast_row, 0.0), axis=1, keepdims=True)
        m_prev = m_scr[hd:hd + 1, :1]
        m_t = jnp.maximum(b_col + m_prev, jnp.max(log_d, axis=1, keepdims=True))
        decay = jnp.exp(b_col + m_prev - m_t)
        qh = q_ref[0, :, hd * dk:(hd + 1) * dk]
        kh = k_ref[0, :, hd * dk:(hd + 1) * dk]
        vh = v_ref[0, :, hd * dv:(hd + 1) * dv]
        c_prev = c_scr[hd]
        n_prev = n_scr[hd:hd + 1, :]
        sm = qk_ * jnp.exp(log_d - m_t)
        num = _dot(sm.astype(BF16), vh) + decay * qc
        den = jnp.sum(sm, axis=1, keepdims=True) + decay * jnp.sum(qh.astype(F32) * n_prev, axis=1, keepdims=True)
        hh = num / jnp.maximum(jnp.abs(den), jnp.exp(-m_t))
        hh = hh * lax.rsqrt(jnp.mean(hh * hh, axis=1, keepdims=True) + NORM_EPS) * gh_ref[:, hd * dv:(hd + 1) * dv]
        outs.append((jax.nn.sigmoid(o_ref[0, :, hd * dv:(hd + 1) * dv]) * hh).astype(BF16))
        m_new = m_t[L - 1:L, :]
        wgt = jnp.exp(i_col + b_last - b_col - m_new)
        cdec = jnp.exp(b_last + m_prev - m_new)
        kw_ = kh.astype(F32) * wgt
        c_scr[hd] = cdec * c_prev + _dot_tn(kw_.astype(BF16), vh)
        n_scr[hd:hd + 1, :] = cdec * n_prev + jnp.sum(kw_, axis=0, keepdims=True)
        m_scr[hd:hd + 1, :] = jnp.broadcast_to(m_new, (1, LANES))

    out = jnp.concatenate(outs, axis=1)
    y_ref[0] = x_ref[0] + _dot(out, wo_ref[...])

    @pl.when(ci == pl.num_programs(1) - 1)
    def _():
        c_ref[0] = c_scr[...]
        n_ref[0] = n_scr[...]
        m_ref[0] = m_scr[...]


def _mlstm(q, k, v, o, gt, x, c0, n0, m0, bias, g_head, w_out, chunk=256):
    B, T, D = x.shape
    heads, dk, dv = c0.shape[1], c0.shape[2], c0.shape[3]
    L = min(chunk, T)
    tok = lambda wd: pl.BlockSpec((1, L, wd), lambda b, c: (b, c, 0))
    st3 = pl.BlockSpec((1, heads, LANES), lambda b, c: (b, 0, 0))
    st4 = pl.BlockSpec((1, heads, dk, dv), lambda b, c: (b, 0, 0, 0))
    return pl.pallas_call(
        functools.partial(_mlstm_body, heads=heads, dk=dk, dv=dv),
        out_shape=(jax.ShapeDtypeStruct(x.shape, F32),
                   jax.ShapeDtypeStruct(c0.shape, F32),
                   jax.ShapeDtypeStruct((B, heads, dk), F32),
                   jax.ShapeDtypeStruct((B, heads, LANES), F32)),
        grid=(B, T // L),
        in_specs=[tok(heads * dk), tok(heads * dk), tok(heads * dv), tok(heads * dv), tok(LANES), tok(D),
                  st4, st3, st3, _full((1, LANES)), _full((1, heads * dv)), _full(w_out.shape)],
        out_specs=(tok(D), st4, st3, st3),
        scratch_shapes=[pltpu.VMEM((heads, dk, dv), F32), pltpu.VMEM((heads, dk), F32),
                        pltpu.VMEM((heads, LANES), F32)],
        compiler_params=_cparams("parallel", "arbitrary"),
        name="mlstm",
    )(q, k, v, o, gt, x, c0, n0, m0, bias, g_head.reshape(1, heads * dv), w_out)


def kernel(x_prompt, x_sample, cache_ckv, cache_krope, state_C, state_n, state_m, page_table, p_prompt, p_sample, g_ffa, w_ffa_in, w_ffa_out, g_mix, g_ffb, w_ffb_in, w_ffb_out, g_ple, w_ple_gate, w_ple_proj, w_mla_in, g_q_lat, g_kv_lat, w_uq, w_uk, w_uv, w_mla_out, w_mlstm_in, b_igate, b_fgate, g_mlstm_head, w_mlstm_out, g_final):
    depth = g_ffa.shape[0]
    seq, dec_seq = x_prompt.shape[1], x_sample.shape[1]
    past = page_table.shape[1] * cache_ckv.shape[2]
    positions = (jnp.arange(seq), past + jnp.arange(dec_seq))
    xs = [x_prompt, x_sample]
    ps = (p_prompt, p_sample)
    heads_m, dk, dv = state_C.shape[2], state_C.shape[3], state_C.shape[4]
    assert dk == LANES and 2 * heads_m <= LANES
    qk, hv = heads_m * dk, heads_m * dv
    outs = {name: ([], []) for name in ("ckv", "kr", "C", "n", "m")}
    krope_t = jnp.swapaxes(cache_krope, 2, 3)

    for i in range(depth):
        j = i // 2
        bf = lambda a: a.astype(BF16)
        w_ffa = (bf(w_ffa_in[i]), bf(w_ffa_out[i]))
        w_ffb = (bf(w_ffb_in[i]), bf(w_ffb_out[i]))
        w_gate, w_proj = bf(w_ple_gate[i]), bf(w_ple_proj[i])
        if i % 2 == 0:
            mla = _prep_mla(w_mla_in[j], g_q_lat[j], g_kv_lat[j], w_uq[j], w_uk[j], w_uv[j], w_mla_out[j])
        else:
            wi = w_mlstm_in[j]
            pad = LANES - 2 * heads_m
            w_ml = bf(jnp.pad(wi, ((0, 0), (0, pad))))
            bias = jnp.pad(jnp.concatenate([b_igate[j], b_fgate[j]]), (0, pad)).reshape(1, LANES)
            w_mo = bf(w_mlstm_out[j])
        for g in range(2):
            x = xs[g]
            nb = x.shape[0]
            x = _ffn(x, g_ffa[i], *w_ffa)
            if i % 2 == 0:
                tab = _rope_table(positions[g], mla["rope"])
                q, k, ckv, kr = _mla_proj(x, g_mix[i], mla, tab)
                if g == 0:
                    o = _attn_prompt(q, k, mla["kv_rank"])
                else:
                    o = _attn_decode(q, k, cache_ckv, krope_t, j, page_table, mla["kv_rank"], mla["rope"])
                x = _mla_out(o, x, mla)
                outs["ckv"][g].append(ckv)
                outs["kr"][g].append(kr)
            else:
                if g == 0:
                    c0 = jnp.zeros((nb, heads_m, dk, dv), F32)
                    n0 = jnp.zeros((nb, heads_m, dk), F32)
                    m0 = jnp.zeros((nb, heads_m), F32)
                else:
                    c0, n0, m0 = state_C[j], state_n[j], state_m[j]
                m0 = jnp.broadcast_to(m0[:, :, None], (nb, heads_m, LANES))
                qm, km, vm, om, gt = _mlstm_proj(x, g_mix[i], w_ml, qk, hv, dk)
                x, c_new, n_new, m_new = _mlstm(qm, km, vm, om, gt, x, c0, n0, m0, bias, g_mlstm_head[j], w_mo)
                outs["C"][g].append(c_new)
                outs["n"][g].append(n_new)
                outs["m"][g].append(m_new[:, :, 0])
            x = _ffn(x, g_ffb[i], *w_ffb)
            x = _ple(x, ps[g][i], g_ple[i], w_gate, w_proj, g_final, final=(i == depth - 1))
            xs[g] = x

    st = lambda name, g: jnp.stack(outs[name][g])
    return (xs[0], xs[1], st("ckv", 0), st("kr", 0), st("ckv", 1), st("kr", 1),
            st("C", 0), st("n", 0), st("m", 0), st("C", 1), st("n", 1), st("m", 1))
```

```python
import functools
import math

import jax
import jax.numpy as jnp
import numpy as np
from jax import lax
from jax.experimental import pallas as pl
from jax.experimental.pallas import tpu as pltpu

NORM_EPS = 1e-6
ROPE_BASE = 10000.0
GATE_CAP = 15.0

LANES = 128
MXU_TILE = 256
VMEM_LIMIT = 56 * 1024 * 1024

F32 = jnp.float32
BF16 = jnp.bfloat16
HIGHEST = lax.Precision.HIGHEST


def _cparams(*sem):
    return pltpu.CompilerParams(dimension_semantics=sem, vmem_limit_bytes=VMEM_LIMIT)


def _dot(a, b):
    return jnp.dot(a, b, preferred_element_type=F32)


def _dot_nt(a, b, precision=None):
    return lax.dot_general(a, b, (((1,), (1,)), ((), ())), precision=precision,
                           preferred_element_type=F32)


def _dot_tn(a, b):
    return lax.dot_general(a, b, (((0,), (0,)), ((), ())), preferred_element_type=F32)


def _rmsnorm(x, g):
    return x * lax.rsqrt(jnp.mean(x * x, axis=-1, keepdims=True) + NORM_EPS) * g


def _full(shape):
    zeros = (0,) * len(shape)
    return pl.BlockSpec(shape, lambda *_: zeros)


def _tok_spec(bb, tt, width):
    return pl.BlockSpec((bb, tt, width), lambda b, t, *_: (b, t, 0))


def _token_tiles(B, T, rows):
    if T >= rows:
        return 1, rows
    return min(B, rows // T), T


def _ffn_body(x_ref, g_ref, win_ref, wo_ref, o_ref, *, sub):
    bb, tt, d = x_ref.shape
    f = wo_ref.shape[0]
    nb = sub // tt if bb > 1 else 1
    for r in range(bb * tt // sub):
        if bb == 1:
            x = x_ref[0, r * sub:(r + 1) * sub, :]
        else:
            x = x_ref[r * nb:(r + 1) * nb].reshape(sub, d)
        h = _rmsnorm(x, g_ref[...]).astype(BF16)
        a = _dot(h, win_ref[:, :f])
        b = _dot(h, win_ref[:, f:])
        act = (a * jax.nn.sigmoid(a) * b).astype(BF16)
        y = x + 0.5 * _dot(act, wo_ref[...])
        if bb == 1:
            o_ref[0, r * sub:(r + 1) * sub, :] = y
        else:
            o_ref[r * nb:(r + 1) * nb] = y.reshape(nb, tt, d)


def _ffn(x, g, w_in, w_out, rows=512, sub=256):
    B, T, D = x.shape
    F = w_out.shape[0]
    bb, tt = _token_tiles(B, T, rows)
    sub = min(sub, bb * tt)
    resident = pl.Buffered(1)
    return pl.pallas_call(
        functools.partial(_ffn_body, sub=sub),
        out_shape=jax.ShapeDtypeStruct(x.shape, F32),
        grid=(B // bb, T // tt),
        in_specs=[
            _tok_spec(bb, tt, D),
            _full((1, D)),
            pl.BlockSpec((D, 2 * F), lambda b, t: (0, 0), pipeline_mode=resident),
            pl.BlockSpec((F, D), lambda b, t: (0, 0), pipeline_mode=resident),
        ],
        out_specs=_tok_spec(bb, tt, D),
        compiler_params=_cparams("parallel", "parallel"),
        name="ffn",
    )(x, g.reshape(1, D), w_in, w_out)


def _ple_body(x_ref, p_ref, g_ref, wg_ref, wp_ref, gf_ref, o_ref, *, final):
    rows = x_ref.shape[0] * x_ref.shape[1]
    x = x_ref[...].reshape(rows, -1)
    p = p_ref[...].reshape(rows, -1).astype(BF16)
    hn = _rmsnorm(x, g_ref[...]).astype(BF16)
    gate = jax.nn.sigmoid(_dot(hn, wg_ref[...]))
    y = x + gate * _dot(p, wp_ref[...])
    if final:
        y = _rmsnorm(y, gf_ref[...])
    o_ref[...] = y.reshape(o_ref.shape)


def _ple(x, p, g, w_gate, w_proj, g_final, final, rows=512):
    B, T, D = x.shape
    P = p.shape[-1]
    bb, tt = _token_tiles(B, T, rows)
    return pl.pallas_call(
        functools.partial(_ple_body, final=final),
        out_shape=jax.ShapeDtypeStruct(x.shape, F32),
        grid=(B // bb, T // tt),
        in_specs=[_tok_spec(bb, tt, D), _tok_spec(bb, tt, P), _full((1, D)),
                  _full((D, D)), _full((P, D)), _full((1, D))],
        out_specs=_tok_spec(bb, tt, D),
        compiler_params=_cparams("parallel", "parallel"),
        name="ple",
    )(x, p, g.reshape(1, D), w_gate, w_proj, g_final.reshape(1, D))


def _mla_proj_body(x_ref, gm_ref, win_ref, gq_ref, gkv_ref, wqn_ref, wqr_ref, wuk_ref, tab_ref,
                   q_ref, k_ref, ckv_ref, kr_ref, *, scale, q_rank, kv_rank, rope):
    bb, tt = x_ref.shape[0], x_ref.shape[1]
    rows = bb * tt
    heads = q_ref.shape[1]
    x = x_ref[...].reshape(rows, -1)
    h = _rmsnorm(x, gm_ref[...]).astype(BF16)
    z = _dot(h, win_ref[...])
    cq = _rmsnorm(z[:, :q_rank], gq_ref[...]).astype(BF16)
    ckv = _rmsnorm(z[:, q_rank:q_rank + kv_rank], gkv_ref[...])
    tab = jnp.broadcast_to(tab_ref[...][None], (bb, tt, LANES)).reshape(rows, LANES)
    yk = z[:, q_rank + kv_rank:] * tab
    kr = yk + pltpu.roll(yk, LANES - rope, 1)
    lane = lax.broadcasted_iota(jnp.int32, (rows, LANES), 1)
    kdup = jnp.where(lane < rope, kr, jnp.where(lane < 2 * rope, pltpu.roll(kr, rope, 1), 0.0))
    ckv_ref[...] = ckv.reshape(ckv_ref.shape)
    kr_ref[...] = kr[:, :rope].reshape(kr_ref.shape)
    k_ref[:, :, :kv_rank] = ckv.astype(BF16).reshape(bb, tt, kv_rank)
    k_ref[:, :, kv_rank:] = kdup.astype(BF16).reshape(bb, tt, LANES)

    qn = _dot(cq, wqn_ref[...]).astype(BF16)
    per = heads // (qn.shape[1] // MXU_TILE)
    for hd in range(heads):
        c0 = (hd // per) * MXU_TILE
        qa = _dot(qn[:, c0:c0 + MXU_TILE], wuk_ref[hd]) * scale
        qy = _dot(cq, wqr_ref[hd]) * tab * scale
        q_ref[:, hd, :, :kv_rank] = qa.astype(BF16).reshape(bb, tt, kv_rank)
        q_ref[:, hd, :, kv_rank:] = qy.astype(BF16).reshape(bb, tt, LANES)


def _mla_proj(x, g_mix, w, tab, rows=256):
    B, T, D = x.shape
    heads, q_rank, kv_rank, rope = w["heads"], w["q_rank"], w["kv_rank"], w["rope"]
    bb, tt = _token_tiles(B, T, rows)
    kw = kv_rank + LANES
    body = functools.partial(_mla_proj_body, scale=w["scale"], q_rank=q_rank, kv_rank=kv_rank, rope=rope)
    return pl.pallas_call(
        body,
        out_shape=(
            jax.ShapeDtypeStruct((B, heads, T, kw), BF16),
            jax.ShapeDtypeStruct((B, T, kw), BF16),
            jax.ShapeDtypeStruct((B, T, kv_rank), F32),
            jax.ShapeDtypeStruct((B, T, rope), F32),
        ),
        grid=(B // bb, T // tt),
        in_specs=[
            _tok_spec(bb, tt, D), _full((1, D)), _full(w["w_in"].shape), _full((1, q_rank)),
            _full((1, kv_rank)), _full(w["w_qn"].shape), _full(w["w_qr"].shape), _full(w["w_uk"].shape),
            pl.BlockSpec((tt, LANES), lambda b, t: (t, 0)),
        ],
        out_specs=(
            pl.BlockSpec((bb, heads, tt, kw), lambda b, t: (b, 0, t, 0)),
            _tok_spec(bb, tt, kw), _tok_spec(bb, tt, kv_rank), _tok_spec(bb, tt, rope),
        ),
        compiler_params=_cparams("parallel", "parallel"),
        name="mla_proj",
    )(x, g_mix.reshape(1, D), w["w_in"], w["g_q"].reshape(1, q_rank), w["g_kv"].reshape(1, kv_rank),
      w["w_qn"], w["w_qr"], w["w_uk"], tab)


def _prep_mla(w_mla_in, g_q, g_kv, w_uq, w_uk, w_uv, w_out):
    q_rank, heads, qd = w_uq.shape
    kv_rank, _, nope = w_uk.shape
    vdim = w_uv.shape[2]
    rope = qd - nope
    half = rope // 2
    assert kv_rank == MXU_TILE and MXU_TILE % nope == 0 and MXU_TILE % vdim == 0 and 2 * rope <= LANES
    d = w_mla_in.shape[0]
    wr = w_mla_in[:, q_rank + kv_rank:]
    w_in = jnp.concatenate(
        [w_mla_in, wr[:, half:], wr[:, :half], jnp.zeros((d, LANES - 2 * rope), F32)], axis=1).astype(BF16)
    w_qn = w_uq[:, :, :nope].reshape(q_rank, heads * nope).astype(BF16)
    x1, x2 = w_uq[:, :, nope:nope + half], w_uq[:, :, nope + half:]
    w_qr = jnp.concatenate([x1, x2, x2, x1, jnp.zeros((q_rank, heads, LANES - 2 * rope), F32)], axis=2)
    w_qr = jnp.transpose(w_qr, (1, 0, 2)).astype(BF16)
    per = MXU_TILE // nope
    slot = jax.nn.one_hot(jnp.arange(heads) % per, per, dtype=F32)
    w_ukp = jnp.einsum("rhd,hs->hsdr", w_uk, slot).reshape(heads, MXU_TILE, kv_rank).astype(BF16)
    perv = MXU_TILE // vdim
    slotv = jax.nn.one_hot(jnp.arange(heads) % perv, perv, dtype=F32)
    w_uvp = jnp.einsum("rhv,hs->hrsv", w_uv, slotv).reshape(heads, kv_rank, MXU_TILE).astype(BF16)
    return dict(heads=heads, q_rank=q_rank, kv_rank=kv_rank, rope=rope, nope=nope, vdim=vdim,
                scale=float((nope + rope) ** -0.5 * math.log2(math.e)), w_in=w_in, g_q=g_q, g_kv=g_kv, w_qn=w_qn,
                w_qr=w_qr, w_uk=w_ukp, w_uv=w_uvp, w_out=w_out.astype(BF16))


def _rope_table(pos, rope):
    half = rope // 2
    inv = ROPE_BASE ** (-jnp.arange(half, dtype=F32) / half)
    ang = pos.astype(F32)[:, None] * inv[None, :]
    c, s = jnp.cos(ang), jnp.sin(ang)
    return jnp.concatenate([c, c, -s, s, jnp.zeros((pos.shape[0], LANES - 2 * rope), F32)], axis=1)


def _lane_tile(stat, width):
    if width % LANES:
        return stat[:, :1]
    return stat if width == LANES else jnp.tile(stat, (1, width // LANES))


def _softmax_step(s, v, m_scr, l_scr, acc_scr):
    m_prev = m_scr[...]
    m_new = jnp.maximum(m_prev, jnp.max(s, axis=1, keepdims=True))
    alpha = jnp.exp2(m_prev - m_new)
    p = jnp.exp2(s - _lane_tile(m_new, s.shape[1]))
    l_scr[...] = alpha * l_scr[...] + jnp.sum(p, axis=1, keepdims=True)
    acc_scr[...] = acc_scr[...] * _lane_tile(alpha, v.shape[1]) + _dot(p.astype(BF16), v)
    m_scr[...] = m_new


def _softmax_result(l_scr, acc_scr):
    return acc_scr[...] / _lane_tile(l_scr[...], acc_scr.shape[-1])


def _softmax_init(m_scr, l_scr, acc_scr):
    m_scr[...] = jnp.full_like(m_scr, -jnp.inf)
    l_scr[...] = jnp.zeros_like(l_scr)
    acc_scr[...] = jnp.zeros_like(acc_scr)


def _attn_prompt_body(qi_ref, kj_ref, q_ref, k_ref, o_ref, m_scr, l_scr, acc_scr, *, tq, tk, kv_rank):
    s_id = pl.program_id(1)
    qi, kj = qi_ref[s_id], kj_ref[s_id]
    heads = q_ref.shape[1]
    groups, rows = m_scr.shape[0], m_scr.shape[1]
    hb = heads // groups

    @pl.when(kj == 0)
    def _():
        _softmax_init(m_scr, l_scr, acc_scr)

    k = k_ref[0]
    v = k[:, :kv_rank]
    qpos = qi * tq + lax.broadcasted_iota(jnp.int32, (tq, tk), 0)
    kpos = kj * tk + lax.broadcasted_iota(jnp.int32, (tq, tk), 1)
    bias = jnp.where(kpos <= qpos, 0.0, -jnp.inf)

    def scores(g):
        q = q_ref[0, g * hb:(g + 1) * hb].reshape(rows, -1)
        return (_dot_nt(q, k).reshape(hb, tq, tk) + bias[None]).reshape(rows, tk)

    s_next = scores(0)
    for g in range(groups):
        s = s_next
        if g + 1 < groups:
            s_next = scores(g + 1)
        _softmax_step(s, v, m_scr.at[g], l_scr.at[g], acc_scr.at[g])

    @pl.when(kj == ((qi + 1) * tq - 1) // tk)
    def _():
        for g in range(groups):
            o = _softmax_result(l_scr.at[g], acc_scr.at[g]).astype(BF16)
            o_ref[0, g * hb:(g + 1) * hb] = o.reshape(hb, tq, kv_rank)


def _attn_prompt(q, k, kv_rank, tq=128, tk=512, hb=2):
    B, heads, T, kw = q.shape
    tq, tk = min(tq, T), min(tk, T)
    groups, rows = heads // hb, hb * tq
    pairs = [(i, j) for i in range(T // tq) for j in range(((i + 1) * tq - 1) // tk + 1)]
    qi = jnp.asarray(np.array([p[0] for p in pairs], np.int32))
    kj = jnp.asarray(np.array([p[1] for p in pairs], np.int32))
    grid_spec = pltpu.PrefetchScalarGridSpec(
        num_scalar_prefetch=2,
        grid=(B, len(pairs)),
        in_specs=[
            pl.BlockSpec((1, heads, tq, kw), lambda b, s, qi, kj: (b, 0, qi[s], 0)),
            pl.BlockSpec((1, tk, kw), lambda b, s, qi, kj: (b, kj[s], 0)),
        ],
        out_specs=pl.BlockSpec((1, heads, tq, kv_rank), lambda b, s, qi, kj: (b, 0, qi[s], 0)),
        scratch_shapes=[pltpu.VMEM((groups, rows, LANES), F32), pltpu.VMEM((groups, rows, LANES), F32),
                        pltpu.VMEM((groups, rows, kv_rank), F32)],
    )
    return pl.pallas_call(
        functools.partial(_attn_prompt_body, tq=tq, tk=tk, kv_rank=kv_rank),
        out_shape=jax.ShapeDtypeStruct((B, heads, T, kv_rank), BF16),
        grid_spec=grid_spec,
        compiler_params=_cparams("parallel", "arbitrary"),
        name="attn_prompt",
    )(qi, kj, q, k)


def _attn_decode_body(pt_ref, q_ref, kn_ref, ck_hbm, kr_hbm, o_ref, ck_buf, kr_buf, sem, m_scr, l_scr, acc_scr,
                      *, layer, kv_rank, rope):
    step, nsteps = pl.program_id(0), pl.num_programs(0)
    nb, heads, tn = q_ref.shape[0], q_ref.shape[1], q_ref.shape[2]
    pages, page = ck_buf.shape[2], ck_buf.shape[3]
    n_chunks = pt_ref.shape[1] // pages
    rows = heads * tn

    def page_copies(slot, page_id):
        copies = []
        for e in range(nb):
            for j in range(pages):
                pid = page_id(e, j)
                copies.append(pltpu.make_async_copy(ck_hbm.at[layer, pid], ck_buf.at[slot, e, j], sem.at[slot]))
                copies.append(pltpu.make_async_copy(kr_hbm.at[layer, pid], kr_buf.at[slot, e, j], sem.at[slot]))
        return copies

    def start_chunk(slot, at_step, chunk):
        for c in page_copies(slot, lambda e, j: pt_ref[at_step * nb + e, chunk * pages + j]):
            c.start()

    def wait_chunk(slot):
        for c in page_copies(slot, lambda e, j: 0):
            c.wait()

    @pl.when(step == 0)
    def _():
        start_chunk(0, 0, 0)

    _softmax_init(m_scr, l_scr, acc_scr)

    def queries(e):
        q = q_ref[e].reshape(rows, -1)
        qy = q[:, kv_rank:].astype(F32)
        qr = (qy + pltpu.roll(qy, LANES - rope, 1))[:, :rope].astype(BF16)
        return q, q[:, :kv_rank], qr

    def attend(slot):
        scores = []
        for e in range(nb):
            _, qa, qr = queries(e)
            kc = ck_buf[slot, e].reshape(pages * page, kv_rank).astype(BF16)
            krt = jnp.concatenate([kr_buf[slot, e, j].astype(BF16) for j in range(pages)], axis=1)
            scores.append((_dot_nt(qa, kc) + _dot(qr, krt), kc))
        for e, (s, kc) in enumerate(scores):
            _softmax_step(s, kc, m_scr.at[e], l_scr.at[e], acc_scr.at[e])

    def chunk_pair(c2, carry):
        start_chunk(1, step, 2 * c2 + 1)
        wait_chunk(0)
        attend(0)
        last = c2 == n_chunks // 2 - 1
        nxt_step = jnp.where(last, step + 1, step)
        nxt_chunk = jnp.where(last, 0, 2 * c2 + 2)

        @pl.when(nxt_step < nsteps)
        def _():
            start_chunk(0, nxt_step, nxt_chunk)

        wait_chunk(1)
        attend(1)
        return carry

    lax.fori_loop(0, n_chunks // 2, chunk_pair, 0)

    for e in range(nb):
        q, _, _ = queries(e)
        kn = kn_ref[e]
        s = _dot_nt(q, kn)
        tq_ = lax.broadcasted_iota(jnp.int32, (heads, tn, tn), 1).reshape(rows, tn)
        tk_ = lax.broadcasted_iota(jnp.int32, (rows, tn), 1)
        s = jnp.where(tk_ <= tq_, s, -jnp.inf)
        _softmax_step(s, kn[:, :kv_rank], m_scr.at[e], l_scr.at[e], acc_scr.at[e])
        o = _softmax_result(l_scr.at[e], acc_scr.at[e])
        o_ref[e] = o.astype(BF16).reshape(heads, tn, kv_rank)


def _attn_decode(q, k_new, cache_ckv, cache_krope_t, layer, page_table, kv_rank, rope, pages=16, nb=4):
    B, heads, tn, kw = q.shape
    n_pages = page_table.shape[1]
    page = cache_ckv.shape[2]
    nb = math.gcd(nb, B)
    pages = math.gcd(pages, n_pages // 2)
    assert (n_pages // pages) % 2 == 0, "the two-slot ring walks chunks in pairs"
    rows = heads * tn
    grid_spec = pltpu.PrefetchScalarGridSpec(
        num_scalar_prefetch=1,
        grid=(B // nb,),
        in_specs=[pl.BlockSpec((nb, heads, tn, kw), lambda b, pt: (b, 0, 0, 0)),
                  pl.BlockSpec((nb, tn, kw), lambda b, pt: (b, 0, 0)),
                  pl.BlockSpec(memory_space=pl.ANY),
                  pl.BlockSpec(memory_space=pl.ANY)],
        out_specs=pl.BlockSpec((nb, heads, tn, kv_rank), lambda b, pt: (b, 0, 0, 0)),
        scratch_shapes=[pltpu.VMEM((2, nb, pages, page, kv_rank), cache_ckv.dtype),
                        pltpu.VMEM((2, nb, pages, rope, page), cache_krope_t.dtype),
                        pltpu.SemaphoreType.DMA((2,)),
                        pltpu.VMEM((nb, rows, LANES), F32), pltpu.VMEM((nb, rows, LANES), F32),
                        pltpu.VMEM((nb, rows, kv_rank), F32)],
    )
    return pl.pallas_call(
        functools.partial(_attn_decode_body, layer=layer, kv_rank=kv_rank, rope=rope),
        out_shape=jax.ShapeDtypeStruct((B, heads, tn, kv_rank), BF16),
        grid_spec=grid_spec,
        compiler_params=_cparams("arbitrary"),
        name="attn_decode",
    )(page_table, q, k_new, cache_ckv, cache_krope_t)


def _mla_out_body(o_ref, x_ref, wuv_ref, wo_ref, y_ref):
    bb, heads, tt, kv_rank = o_ref.shape
    rows = bb * tt
    groups = wo_ref.shape[0] // MXU_TILE
    per = heads // groups
    vs = []
    for g in range(groups):
        acc = None
        for hd in range(g * per, (g + 1) * per):
            part = _dot(o_ref[:, hd].reshape(rows, kv_rank), wuv_ref[hd])
            acc = part if acc is None else acc + part
        vs.append(acc.astype(BF16))
    v = jnp.concatenate(vs, axis=1)
    x = x_ref[...].reshape(rows, -1)
    y_ref[...] = (x + _dot(v, wo_ref[...])).reshape(y_ref.shape)


def _mla_out(o, x, w, rows=256):
    B, T, D = x.shape
    heads, kv_rank = o.shape[1], o.shape[3]
    bb, tt = _token_tiles(B, T, rows)
    return pl.pallas_call(
        _mla_out_body,
        out_shape=jax.ShapeDtypeStruct(x.shape, F32),
        grid=(B // bb, T // tt),
        in_specs=[pl.BlockSpec((bb, heads, tt, kv_rank), lambda b, t: (b, 0, t, 0)),
                  _tok_spec(bb, tt, D), _full(w["w_uv"].shape), _full(w["w_out"].shape)],
        out_specs=_tok_spec(bb, tt, D),
        compiler_params=_cparams("parallel", "parallel"),
        name="mla_out",
    )(o, x, w["w_uv"], w["w_out"])


def _mlstm_proj_body(x_ref, g_ref, w_ref, q_ref, k_ref, v_ref, o_ref, gt_ref, *, qk, hv, qscale):
    rows = x_ref.shape[0] * x_ref.shape[1]
    x = x_ref[...].reshape(rows, -1)
    h = _rmsnorm(x, g_ref[...]).astype(BF16)
    z = _dot(h, w_ref[...])
    q_ref[...] = (z[:, :qk] * qscale).astype(BF16).reshape(q_ref.shape)
    k_ref[...] = z[:, qk:2 * qk].astype(BF16).reshape(k_ref.shape)
    v_ref[...] = z[:, 2 * qk:2 * qk + hv].astype(BF16).reshape(v_ref.shape)
    o_ref[...] = z[:, 2 * qk + hv:2 * qk + 2 * hv].reshape(o_ref.shape)
    gt_ref[...] = z[:, 2 * qk + 2 * hv:].reshape(gt_ref.shape)


def _mlstm_proj(x, g, w, qk, hv, dk, rows=512):
    B, T, D = x.shape
    bb, tt = _token_tiles(B, T, rows)
    widths = (qk, qk, hv, hv, LANES)
    dtypes = (BF16, BF16, BF16, F32, F32)
    return pl.pallas_call(
        functools.partial(_mlstm_proj_body, qk=qk, hv=hv, qscale=float(dk ** -0.5)),
        out_shape=tuple(jax.ShapeDtypeStruct((B, T, wd), dt) for wd, dt in zip(widths, dtypes)),
        grid=(B // bb, T // tt),
        in_specs=[_tok_spec(bb, tt, D), _full((1, D)), _full(w.shape)],
        out_specs=tuple(_tok_spec(bb, tt, wd) for wd in widths),
        compiler_params=_cparams("parallel", "parallel"),
        name="mlstm_proj",
    )(x, g.reshape(1, D), w)


def _mlstm_body(q_ref, k_ref, v_ref, o_ref, gt_ref, x_ref, c0_ref, n0_ref, m0_ref, bias_ref, gh_ref, wo_ref,
                y_ref, c_ref, n_ref, m_ref, c_scr, n_scr, m_scr, *, heads, dk, dv):
    ci = pl.program_id(1)
    L = q_ref.shape[1]

    @pl.when(ci == 0)
    def _():
        c_scr[...] = c0_ref[0]
        n_scr[...] = n0_ref[0]
        m_scr[...] = m0_ref[0]

    a = GATE_CAP * jnp.tanh((gt_ref[0] + bias_ref[...]) / GATE_CAP)
    lf = -(jnp.maximum(-a, 0.0) + jnp.log1p(jnp.exp(-jnp.abs(a))))
    lane = lax.broadcasted_iota(jnp.int32, (L, LANES), 1)
    xg = jnp.where(lane < heads, a, jnp.where(lane < 2 * heads, lf, 0.0))
    row = lax.broadcasted_iota(jnp.int32, (L, L), 0)
    col = lax.broadcasted_iota(jnp.int32, (L, L), 1)
    causal = col <= row
    bc = jnp.dot(causal.astype(F32), xg, precision=HIGHEST, preferred_element_type=F32)
    blast_row = bc[L - 1:L, :]
    xg_t, bc_t = xg.T, bc.T

    pre = []
    for hd in range(heads):
        qh = q_ref[0, :, hd * dk:(hd + 1) * dk]
        kh = k_ref[0, :, hd * dk:(hd + 1) * dk]
        pre.append((_dot_nt(qh, kh), _dot(qh, c_scr[hd].astype(BF16))))

    outs = []
    for hd in range(heads):
        qk_, qc = pre[hd]
        b_col = jnp.sum(jnp.where(lane == heads + hd, bc, 0.0), axis=1, keepdims=True)
        ib_row = xg_t[hd:hd + 1, :] - bc_t[heads + hd:heads + hd + 1, :]
        log_d = jnp.where(causal, b_col + ib_row, -jnp.inf)
        i_col = jnp.sum(jnp.where(lane == hd, xg, 0.0), axis=1, keepdims=True)
        b_last = jnp.sum(jnp.where(lane[:1] == heads + hd, blast_row, 0.0), axis=1, keepdims=True)
        m_prev = m_scr[hd:hd + 1, :1]
        m_t = jnp.maximum(b_col + m_prev, jnp.max(log_d, axis=1, keepdims=True))
        decay = jnp.exp(b_col + m_prev - m_t)
        qh = q_ref[0, :, hd * dk:(hd + 1) * dk]
        kh = k_ref[0, :, hd * dk:(hd + 1) * dk]
        vh = v_ref[0, :, hd * dv:(hd + 1) * dv]
        c_prev = c_scr[hd]
        n_prev = n_scr[hd:hd + 1, :]
        sm = qk_ * jnp.exp(log_d - m_t)
        num = _dot(sm.astype(BF16), vh) + decay * qc
        den = jnp.sum(sm, axis=1, keepdims=True) + decay * jnp.sum(qh.astype(F32) * n_prev, axis=1, keepdims=True)
        hh = num / jnp.maximum(jnp.abs(den), jnp.exp(-m_t))
        hh = hh * lax.rsqrt(jnp.mean(hh * hh, axis=1, keepdims=True) + NORM_EPS) * gh_ref[:, hd * dv:(hd + 1) * dv]
        outs.append((jax.nn.sigmoid(o_ref[0, :, hd * dv:(hd + 1) * dv]) * hh).astype(BF16))
        m_new = m_t[L - 1:L, :]
        wgt = jnp.exp(i_col + b_last - b_col - m_new)
        cdec = jnp.exp(b_last + m_prev - m_new)
        kw_ = kh.astype(F32) * wgt
        c_scr[hd] = cdec * c_prev + _dot_tn(kw_.astype(BF16), vh)
        n_scr[hd:hd + 1, :] = cdec * n_prev + jnp.sum(kw_, axis=0, keepdims=True)
        m_scr[hd:hd + 1, :] = jnp.broadcast_to(m_new, (1, LANES))

    out = jnp.concatenate(outs, axis=1)
    y_ref[0] = x_ref[0] + _dot(out, wo_ref[...])

    @pl.when(ci == pl.num_programs(1) - 1)
    def _():
        c_ref[0] = c_scr[...]
        n_ref[0] = n_scr[...]
        m_ref[0] = m_scr[...]


def _mlstm(q, k, v, o, gt, x, c0, n0, m0, bias, g_head, w_out, chunk=256):
    B, T, D = x.shape
    heads, dk, dv = c0.shape[1], c0.shape[2], c0.shape[3]
    L = min(chunk, T)
    tok = lambda wd: pl.BlockSpec((1, L, wd), lambda b, c: (b, c, 0))
    st3 = pl.BlockSpec((1, heads, LANES), lambda b, c: (b, 0, 0))
    st4 = pl.BlockSpec((1, heads, dk, dv), lambda b, c: (b, 0, 0, 0))
    return pl.pallas_call(
        functools.partial(_mlstm_body, heads=heads, dk=dk, dv=dv),
        out_shape=(jax.ShapeDtypeStruct(x.shape, F32),
                   jax.ShapeDtypeStruct(c0.shape, F32),
                   jax.ShapeDtypeStruct((B, heads, dk), F32),
                   jax.ShapeDtypeStruct((B, heads, LANES), F32)),
        grid=(B, T // L),
        in_specs=[tok(heads * dk), tok(heads * dk), tok(heads * dv), tok(heads * dv), tok(LANES), tok(D),
                  st4, st3, st3, _full((1, LANES)), _full((1, heads * dv)), _full(w_out.shape)],
        out_specs=(tok(D), st4, st3, st3),
        scratch_shapes=[pltpu.VMEM((heads, dk, dv), F32), pltpu.VMEM((heads, dk), F32),
                        pltpu.VMEM((heads, LANES), F32)],
        compiler_params=_cparams("parallel", "arbitrary"),
        name="mlstm",
    )(q, k, v, o, gt, x, c0, n0, m0, bias, g_head.reshape(1, heads * dv), w_out)


def kernel(x_prompt, x_sample, cache_ckv, cache_krope, state_C, state_n, state_m, page_table, p_prompt, p_sample, g_ffa, w_ffa_in, w_ffa_out, g_mix, g_ffb, w_ffb_in, w_ffb_out, g_ple, w_ple_gate, w_ple_proj, w_mla_in, g_q_lat, g_kv_lat, w_uq, w_uk, w_uv, w_mla_out, w_mlstm_in, b_igate, b_fgate, g_mlstm_head, w_mlstm_out, g_final):
    depth = g_ffa.shape[0]
    seq, dec_seq = x_prompt.shape[1], x_sample.shape[1]
    past = page_table.shape[1] * cache_ckv.shape[2]
    positions = (jnp.arange(seq), past + jnp.arange(dec_seq))
    xs = [x_prompt, x_sample]
    ps = (p_prompt, p_sample)
    heads_m, dk, dv = state_C.shape[2], state_C.shape[3], state_C.shape[4]
    assert dk == LANES and 2 * heads_m <= LANES
    qk, hv = heads_m * dk, heads_m * dv
    outs = {name: ([], []) for name in ("ckv", "kr", "C", "n", "m")}
    krope_t = jnp.swapaxes(cache_krope, 2, 3)

    for i in range(depth):
        j = i // 2
        bf = lambda a: a.astype(BF16)
        w_ffa = (bf(w_ffa_in[i]), bf(w_ffa_out[i]))
        w_ffb = (bf(w_ffb_in[i]), bf(w_ffb_out[i]))
        w_gate, w_proj = bf(w_ple_gate[i]), bf(w_ple_proj[i])
        if i % 2 == 0:
            mla = _prep_mla(w_mla_in[j], g_q_lat[j], g_kv_lat[j], w_uq[j], w_uk[j], w_uv[j], w_mla_out[j])
        else:
            wi = w_mlstm_in[j]
            pad = LANES - 2 * heads_m
            w_ml = bf(jnp.pad(wi, ((0, 0), (0, pad))))
            bias = jnp.pad(jnp.concatenate([b_igate[j], b_fgate[j]]), (0, pad)).reshape(1, LANES)
            w_mo = bf(w_mlstm_out[j])
        for g in range(2):
            x = xs[g]
            nb = x.shape[0]
            x = _ffn(x, g_ffa[i], *w_ffa)
            if i % 2 == 0:
                tab = _rope_table(positions[g], mla["rope"])
                q, k, ckv, kr = _mla_proj(x, g_mix[i], mla, tab)
                if g == 0:
                    o = _attn_prompt(q, k, mla["kv_rank"])
                else:
                    o = _attn_decode(q, k, cache_ckv, krope_t, j, page_table, mla["kv_rank"], mla["rope"])
                x = _mla_out(o, x, mla)
                outs["ckv"][g].append(ckv)
                outs["kr"][g].append(kr)
            else:
                if g == 0:
                    c0 = jnp.zeros((nb, heads_m, dk, dv), F32)
                    n0 = jnp.zeros((nb, heads_m, dk), F32)
                    m0 = jnp.zeros((nb, heads_m), F32)
                else:
                    c0, n0, m0 = state_C[j], state_n[j], state_m[j]
                m0 = jnp.broadcast_to(m0[:, :, None], (nb, heads_m, LANES))
                qm, km, vm, om, gt = _mlstm_proj(x, g_mix[i], w_ml, qk, hv, dk)
                x, c_new, n_new, m_new = _mlstm(qm, km, vm, om, gt, x, c0, n0, m0, bias, g_mlstm_head[j], w_mo)
                outs["C"][g].append(c_new)
                outs["n"][g].append(n_new)
                outs["m"][g].append(m_new[:, :, 0])
            x = _ffn(x, g_ffb[i], *w_ffb)
            x = _ple(x, ps[g][i], g_ple[i], w_gate, w_proj, g_final, final=(i == depth - 1))
            xs[g] = x

    st = lambda name, g: jnp.stack(outs[name][g])
    return (xs[0], xs[1], st("ckv", 0), st("kr", 0), st("ckv", 1), st("kr", 1),
            st("C", 0), st("n", 0), st("m", 0), st("C", 1), st("n", 1), st("m", 1))
```

```python
import functools
import math

import jax
import jax.numpy as jnp
import numpy as np
from jax import lax
from jax.experimental import pallas as pl
from jax.experimental.pallas import tpu as pltpu

NORM_EPS = 1e-6
ROPE_BASE = 10000.0
GATE_CAP = 15.0

LANES = 128
MXU_TILE = 256
VMEM_LIMIT = 56 * 1024 * 1024

F32 = jnp.float32
BF16 = jnp.bfloat16
HIGHEST = lax.Precision.HIGHEST


def _cparams(*sem):
    return pltpu.CompilerParams(dimension_semantics=sem, vmem_limit_bytes=VMEM_LIMIT)


def _dot(a, b):
    return jnp.dot(a, b, preferred_element_type=F32)


def _dot_nt(a, b, precision=None):
    return lax.dot_general(a, b, (((1,), (1,)), ((), ())), precision=precision,
                           preferred_element_type=F32)


def _dot_tn(a, b):
    return lax.dot_general(a, b, (((0,), (0,)), ((), ())), preferred_element_type=F32)


def _rmsnorm(x, g):
    return x * lax.rsqrt(jnp.mean(x * x, axis=-1, keepdims=True) + NORM_EPS) * g


def _full(shape):
    zeros = (0,) * len(shape)
    return pl.BlockSpec(shape, lambda *_: zeros)


def _tok_spec(bb, tt, width):
    return pl.BlockSpec((bb, tt, width), lambda b, t, *_: (b, t, 0))


def _token_tiles(B, T, rows):
    if T >= rows:
        return 1, rows
    return min(B, rows // T), T


def _ffn_body(x_ref, g_ref, win_ref, wo_ref, o_ref, *, sub):
    bb, tt, d = x_ref.shape
    f = wo_ref.shape[0]
    nb = sub // tt if bb > 1 else 1
    for r in range(bb * tt // sub):
        if bb == 1:
            x = x_ref[0, r * sub:(r + 1) * sub, :]
        else:
            x = x_ref[r * nb:(r + 1) * nb].reshape(sub, d)
        h = _rmsnorm(x, g_ref[...]).astype(BF16)
        a = _dot(h, win_ref[:, :f])
        b = _dot(h, win_ref[:, f:])
        act = (a * jax.nn.sigmoid(a) * b).astype(BF16)
        y = x + 0.5 * _dot(act, wo_ref[...])
        if bb == 1:
            o_ref[0, r * sub:(r + 1) * sub, :] = y
        else:
            o_ref[r * nb:(r + 1) * nb] = y.reshape(nb, tt, d)


def _ffn(x, g, w_in, w_out, rows=512, sub=256):
    B, T, D = x.shape
    F = w_out.shape[0]
    bb, tt = _token_tiles(B, T, rows)
    sub = min(sub, bb * tt)
    resident = pl.Buffered(1)
    return pl.pallas_call(
        functools.partial(_ffn_body, sub=sub),
        out_shape=jax.ShapeDtypeStruct(x.shape, F32),
        grid=(B // bb, T // tt),
        in_specs=[
            _tok_spec(bb, tt, D),
            _full((1, D)),
            pl.BlockSpec((D, 2 * F), lambda b, t: (0, 0), pipeline_mode=resident),
            pl.BlockSpec((F, D), lambda b, t: (0, 0), pipeline_mode=resident),
        ],
        out_specs=_tok_spec(bb, tt, D),
        compiler_params=_cparams("parallel", "parallel"),
        name="ffn",
    )(x, g.reshape(1, D), w_in, w_out)


def _ple_body(x_ref, p_ref, g_ref, wg_ref, wp_ref, gf_ref, o_ref, *, final):
    rows = x_ref.shape[0] * x_ref.shape[1]
    x = x_ref[...].reshape(rows, -1)
    p = p_ref[...].reshape(rows, -1).astype(BF16)
    hn = _rmsnorm(x, g_ref[...]).astype(BF16)
    gate = jax.nn.sigmoid(_dot(hn, wg_ref[...]))
    y = x + gate * _dot(p, wp_ref[...])
    if final:
        y = _rmsnorm(y, gf_ref[...])
    o_ref[...] = y.reshape(o_ref.shape)


def _ple(x, p, g, w_gate, w_proj, g_final, final, rows=512):
    B, T, D = x.shape
    P = p.shape[-1]
    bb, tt = _token_tiles(B, T, rows)
    return pl.pallas_call(
        functools.partial(_ple_body, final=final),
        out_shape=jax.ShapeDtypeStruct(x.shape, F32),
        grid=(B // bb, T // tt),
        in_specs=[_tok_spec(bb, tt, D), _tok_spec(bb, tt, P), _full((1, D)),
                  _full((D, D)), _full((P, D)), _full((1, D))],
        out_specs=_tok_spec(bb, tt, D),
        compiler_params=_cparams("parallel", "parallel"),
        name="ple",
    )(x, p, g.reshape(1, D), w_gate, w_proj, g_final.reshape(1, D))


def _mla_proj_body(x_ref, gm_ref, win_ref, gq_ref, gkv_ref, wqn_ref, wqr_ref, wuk_ref, tab_ref,
                   q_ref, k_ref, ckv_ref, kr_ref, *, scale, q_rank, kv_rank, rope):
    bb, tt = x_ref.shape[0], x_ref.shape[1]
    rows = bb * tt
    heads = q_ref.shape[1]
    x = x_ref[...].reshape(rows, -1)
    h = _rmsnorm(x, gm_ref[...]).astype(BF16)
    z = _dot(h, win_ref[...])
    cq = _rmsnorm(z[:, :q_rank], gq_ref[...]).astype(BF16)
    ckv = _rmsnorm(z[:, q_rank:q_rank + kv_rank], gkv_ref[...])
    tab = jnp.broadcast_to(tab_ref[...][None], (bb, tt, LANES)).reshape(rows, LANES)
    yk = z[:, q_rank + kv_rank:] * tab
    kr = yk + pltpu.roll(yk, LANES - rope, 1)
    lane = lax.broadcasted_iota(jnp.int32, (rows, LANES), 1)
    kdup = jnp.where(lane < rope, kr, jnp.where(lane < 2 * rope, pltpu.roll(kr, rope, 1), 0.0))
    ckv_ref[...] = ckv.reshape(ckv_ref.shape)
    kr_ref[...] = kr[:, :rope].reshape(kr_ref.shape)
    k_ref[:, :, :kv_rank] = ckv.astype(BF16).reshape(bb, tt, kv_rank)
    k_ref[:, :, kv_rank:] = kdup.astype(BF16).reshape(bb, tt, LANES)

    qn = _dot(cq, wqn_ref[...]).astype(BF16)
    per = heads // (qn.shape[1] // MXU_TILE)
    for hd in range(heads):
        c0 = (hd // per) * MXU_TILE
        qa = _dot(qn[:, c0:c0 + MXU_TILE], wuk_ref[hd]) * scale
        qy = _dot(cq, wqr_ref[hd]) * tab * scale
        q_ref[:, hd, :, :kv_rank] = qa.astype(BF16).reshape(bb, tt, kv_rank)
        q_ref[:, hd, :, kv_rank:] = qy.astype(BF16).reshape(bb, tt, LANES)


def _mla_proj(x, g_mix, w, tab, rows=256):
    B, T, D = x.shape
    heads, q_rank, kv_rank, rope = w["heads"], w["q_rank"], w["kv_rank"], w["rope"]
    bb, tt = _token_tiles(B, T, rows)
    kw = kv_rank + LANES
    body = functools.partial(_mla_proj_body, scale=w["scale"], q_rank=q_rank, kv_rank=kv_rank, rope=rope)
    return pl.pallas_call(
        body,
        out_shape=(
            jax.ShapeDtypeStruct((B, heads, T, kw), BF16),
            jax.ShapeDtypeStruct((B, T, kw), BF16),
            jax.ShapeDtypeStruct((B, T, kv_rank), F32),
            jax.ShapeDtypeStruct((B, T, rope), F32),
        ),
        grid=(B // bb, T // tt),
        in_specs=[
            _tok_spec(bb, tt, D), _full((1, D)), _full(w["w_in"].shape), _full((1, q_rank)),
            _full((1, kv_rank)), _full(w["w_qn"].shape), _full(w["w_qr"].shape), _full(w["w_uk"].shape),
            pl.BlockSpec((tt, LANES), lambda b, t: (t, 0)),
        ],
        out_specs=(
            pl.BlockSpec((bb, heads, tt, kw), lambda b, t: (b, 0, t, 0)),
            _tok_spec(bb, tt, kw), _tok_spec(bb, tt, kv_rank), _tok_spec(bb, tt, rope),
        ),
        compiler_params=_cparams("parallel", "parallel"),
        name="mla_proj",
    )(x, g_mix.reshape(1, D), w["w_in"], w["g_q"].reshape(1, q_rank), w["g_kv"].reshape(1, kv_rank),
      w["w_qn"], w["w_qr"], w["w_uk"], tab)


def _prep_mla(w_mla_in, g_q, g_kv, w_uq, w_uk, w_uv, w_out):
    q_rank, heads, qd = w_uq.shape
    kv_rank, _, nope = w_uk.shape
    vdim = w_uv.shape[2]
    rope = qd - nope
    half = rope // 2
    assert kv_rank == MXU_TILE and MXU_TILE % nope == 0 and MXU_TILE % vdim == 0 and 2 * rope <= LANES
    d = w_mla_in.shape[0]
    wr = w_mla_in[:, q_rank + kv_rank:]
    w_in = jnp.concatenate(
        [w_mla_in, wr[:, half:], wr[:, :half], jnp.zeros((d, LANES - 2 * rope), F32)], axis=1).astype(BF16)
    w_qn = w_uq[:, :, :nope].reshape(q_rank, heads * nope).astype(BF16)
    x1, x2 = w_uq[:, :, nope:nope + half], w_uq[:, :, nope + half:]
    w_qr = jnp.concatenate([x1, x2, x2, x1, jnp.zeros((q_rank, heads, LANES - 2 * rope), F32)], axis=2)
    w_qr = jnp.transpose(w_qr, (1, 0, 2)).astype(BF16)
    per = MXU_TILE // nope
    slot = jax.nn.one_hot(jnp.arange(heads) % per, per, dtype=F32)
    w_ukp = jnp.einsum("rhd,hs->hsdr", w_uk, slot).reshape(heads, MXU_TILE, kv_rank).astype(BF16)
    perv = MXU_TILE // vdim
    slotv = jax.nn.one_hot(jnp.arange(heads) % perv, perv, dtype=F32)
    w_uvp = jnp.einsum("rhv,hs->hrsv", w_uv, slotv).reshape(heads, kv_rank, MXU_TILE).astype(BF16)
    return dict(heads=heads, q_rank=q_rank, kv_rank=kv_rank, rope=rope, nope=nope, vdim=vdim,
                scale=float((nope + rope) ** -0.5 * math.log2(math.e)), w_in=w_in, g_q=g_q, g_kv=g_kv, w_qn=w_qn,
                w_qr=w_qr, w_uk=w_ukp, w_uv=w_uvp, w_out=w_out.astype(BF16))


def _rope_table(pos, rope):
    half = rope // 2
    inv = ROPE_BASE ** (-jnp.arange(half, dtype=F32) / half)
    ang = pos.astype(F32)[:, None] * inv[None, :]
    c, s = jnp.cos(ang), jnp.sin(ang)
    return jnp.concatenate([c, c, -s, s, jnp.zeros((pos.shape[0], LANES - 2 * rope), F32)], axis=1)


def _lane_tile(stat, width):
    if width % LANES:
        return stat[:, :1]
    return stat if width == LANES else jnp.tile(stat, (1, width // LANES))


def _softmax_step(s, v, m_scr, l_scr, acc_scr):
    m_prev = m_scr[...]
    m_new = jnp.maximum(m_prev, jnp.max(s, axis=1, keepdims=True))
    alpha = jnp.exp2(m_prev - m_new)
    p = jnp.exp2(s - _lane_tile(m_new, s.shape[1]))
    l_scr[...] = alpha * l_scr[...] + jnp.sum(p, axis=1, keepdims=True)
    acc_scr[...] = acc_scr[...] * _lane_tile(alpha, v.shape[1]) + _dot(p.astype(BF16), v)
    m_scr[...] = m_new


def _softmax_result(l_scr, acc_scr):
    return acc_scr[...] / _lane_tile(l_scr[...], acc_scr.shape[-1])


def _softmax_init(m_scr, l_scr, acc_scr):
    m_scr[...] = jnp.full_like(m_scr, -jnp.inf)
    l_scr[...] = jnp.zeros_like(l_scr)
    acc_scr[...] = jnp.zeros_like(acc_scr)


def _attn_prompt_body(qi_ref, kj_ref, q_ref, k_ref, o_ref, m_scr, l_scr, acc_scr, *, tq, tk, kv_rank, ahead):
    s_id = pl.program_id(1)
    qi, kj = qi_ref[s_id], kj_ref[s_id]
    heads = q_ref.shape[1]
    groups, rows = m_scr.shape[0], m_scr.shape[1]
    hb = heads // groups

    @pl.when(kj == 0)
    def _():
        _softmax_init(m_scr, l_scr, acc_scr)

    k = k_ref[0]
    v = k[:, :kv_rank]
    qpos = qi * tq + lax.broadcasted_iota(jnp.int32, (tq, tk), 0)
    kpos = kj * tk + lax.broadcasted_iota(jnp.int32, (tq, tk), 1)
    bias = jnp.where(kpos <= qpos, 0.0, -jnp.inf)

    def scores(g):
        q = q_ref[0, g * hb:(g + 1) * hb].reshape(rows, -1)
        return (_dot_nt(q, k).reshape(hb, tq, tk) + bias[None]).reshape(rows, tk)

    pending = [scores(g) for g in range(min(ahead, groups))]
    for g in range(groups):
        if g + ahead < groups:
            pending.append(scores(g + ahead))
        _softmax_step(pending.pop(0), v, m_scr.at[g], l_scr.at[g], acc_scr.at[g])

    @pl.when(kj == ((qi + 1) * tq - 1) // tk)
    def _():
        for g in range(groups):
            o = _softmax_result(l_scr.at[g], acc_scr.at[g]).astype(BF16)
            o_ref[0, g * hb:(g + 1) * hb] = o.reshape(hb, tq, kv_rank)


def _attn_prompt(q, k, kv_rank, tq=512, tk=512, hb=1, ahead=1):
    B, heads, T, kw = q.shape
    tq, tk = min(tq, T), min(tk, T)
    groups, rows = heads // hb, hb * tq
    pairs = [(i, j) for i in range(T // tq) for j in range(((i + 1) * tq - 1) // tk + 1)]
    qi = jnp.asarray(np.array([p[0] for p in pairs], np.int32))
    kj = jnp.asarray(np.array([p[1] for p in pairs], np.int32))
    grid_spec = pltpu.PrefetchScalarGridSpec(
        num_scalar_prefetch=2,
        grid=(B, len(pairs)),
        in_specs=[
            pl.BlockSpec((1, heads, tq, kw), lambda b, s, qi, kj: (b, 0, qi[s], 0)),
            pl.BlockSpec((1, tk, kw), lambda b, s, qi, kj: (b, kj[s], 0)),
        ],
        out_specs=pl.BlockSpec((1, heads, tq, kv_rank), lambda b, s, qi, kj: (b, 0, qi[s], 0)),
        scratch_shapes=[pltpu.VMEM((groups, rows, LANES), F32), pltpu.VMEM((groups, rows, LANES), F32),
                        pltpu.VMEM((groups, rows, kv_rank), F32)],
    )
    return pl.pallas_call(
        functools.partial(_attn_prompt_body, tq=tq, tk=tk, kv_rank=kv_rank, ahead=ahead),
        out_shape=jax.ShapeDtypeStruct((B, heads, T, kv_rank), BF16),
        grid_spec=grid_spec,
        compiler_params=_cparams("parallel", "arbitrary"),
        name="attn_prompt",
    )(qi, kj, q, k)


def _attn_decode_body(pt_ref, q_ref, kn_ref, ck_hbm, kr_hbm, o_ref, ck_buf, kr_buf, sem, m_scr, l_scr, acc_scr,
                      *, layer, kv_rank, rope):
    step, nsteps = pl.program_id(0), pl.num_programs(0)
    nb, heads, tn = q_ref.shape[0], q_ref.shape[1], q_ref.shape[2]
    pages, page = ck_buf.shape[2], ck_buf.shape[3]
    n_chunks = pt_ref.shape[1] // pages
    rows = heads * tn

    def page_copies(slot, page_id):
        copies = []
        for e in range(nb):
            for j in range(pages):
                pid = page_id(e, j)
                copies.append(pltpu.make_async_copy(ck_hbm.at[layer, pid], ck_buf.at[slot, e, j], sem.at[slot]))
                copies.append(pltpu.make_async_copy(kr_hbm.at[layer, pid], kr_buf.at[slot, e, j], sem.at[slot]))
        return copies

    def start_chunk(slot, at_step, chunk):
        for c in page_copies(slot, lambda e, j: pt_ref[at_step * nb + e, chunk * pages + j]):
            c.start()

    def wait_chunk(slot):
        for c in page_copies(slot, lambda e, j: 0):
            c.wait()

    @pl.when(step == 0)
    def _():
        start_chunk(0, 0, 0)

    _softmax_init(m_scr, l_scr, acc_scr)

    def queries(e):
        q = q_ref[e].reshape(rows, -1)
        qy = q[:, kv_rank:].astype(F32)
        qr = (qy + pltpu.roll(qy, LANES - rope, 1))[:, :rope].astype(BF16)
        return q, q[:, :kv_rank], qr

    def attend(slot):
        scores = []
        for e in range(nb):
            _, qa, qr = queries(e)
            kc = ck_buf[slot, e].reshape(pages * page, kv_rank).astype(BF16)
            krt = jnp.concatenate([kr_buf[slot, e, j].astype(BF16) for j in range(pages)], axis=1)
            scores.append((_dot_nt(qa, kc) + _dot(qr, krt), kc))
        for e, (s, kc) in enumerate(scores):
            _softmax_step(s, kc, m_scr.at[e], l_scr.at[e], acc_scr.at[e])

    def chunk_pair(c2, carry):
        start_chunk(1, step, 2 * c2 + 1)
        wait_chunk(0)
        attend(0)
        last = c2 == n_chunks // 2 - 1
        nxt_step = jnp.where(last, step + 1, step)
        nxt_chunk = jnp.where(last, 0, 2 * c2 + 2)

        @pl.when(nxt_step < nsteps)
        def _():
            start_chunk(0, nxt_step, nxt_chunk)

        wait_chunk(1)
        attend(1)
        return carry

    lax.fori_loop(0, n_chunks // 2, chunk_pair, 0)

    for e in range(nb):
        q, _, _ = queries(e)
        kn = kn_ref[e]
        s = _dot_nt(q, kn)
        tq_ = lax.broadcasted_iota(jnp.int32, (heads, tn, tn), 1).reshape(rows, tn)
        tk_ = lax.broadcasted_iota(jnp.int32, (rows, tn), 1)
        s = jnp.where(tk_ <= tq_, s, -jnp.inf)
        _softmax_step(s, kn[:, :kv_rank], m_scr.at[e], l_scr.at[e], acc_scr.at[e])
        o = _softmax_result(l_scr.at[e], acc_scr.at[e])
        o_ref[e] = o.astype(BF16).reshape(heads, tn, kv_rank)


def _attn_decode(q, k_new, cache_ckv, cache_krope_t, layer, page_table, kv_rank, rope, pages=16, nb=4):
    B, heads, tn, kw = q.shape
    n_pages = page_table.shape[1]
    page = cache_ckv.shape[2]
    nb = math.gcd(nb, B)
    pages = math.gcd(pages, n_pages // 2)
    assert (n_pages // pages) % 2 == 0, "the two-slot ring walks chunks in pairs"
    rows = heads * tn
    grid_spec = pltpu.PrefetchScalarGridSpec(
        num_scalar_prefetch=1,
        grid=(B // nb,),
        in_specs=[pl.BlockSpec((nb, heads, tn, kw), lambda b, pt: (b, 0, 0, 0)),
                  pl.BlockSpec((nb, tn, kw), lambda b, pt: (b, 0, 0)),
                  pl.BlockSpec(memory_space=pl.ANY),
                  pl.BlockSpec(memory_space=pl.ANY)],
        out_specs=pl.BlockSpec((nb, heads, tn, kv_rank), lambda b, pt: (b, 0, 0, 0)),
        scratch_shapes=[pltpu.VMEM((2, nb, pages, page, kv_rank), cache_ckv.dtype),
                        pltpu.VMEM((2, nb, pages, rope, page), cache_krope_t.dtype),
                        pltpu.SemaphoreType.DMA((2,)),
                        pltpu.VMEM((nb, rows, LANES), F32), pltpu.VMEM((nb, rows, LANES), F32),
                        pltpu.VMEM((nb, rows, kv_rank), F32)],
    )
    return pl.pallas_call(
        functools.partial(_attn_decode_body, layer=layer, kv_rank=kv_rank, rope=rope),
        out_shape=jax.ShapeDtypeStruct((B, heads, tn, kv_rank), BF16),
        grid_spec=grid_spec,
        compiler_params=_cparams("arbitrary"),
        name="attn_decode",
    )(page_table, q, k_new, cache_ckv, cache_krope_t)


def _mla_out_body(o_ref, x_ref, wuv_ref, wo_ref, y_ref):
    bb, heads, tt, kv_rank = o_ref.shape
    rows = bb * tt
    groups = wo_ref.shape[0] // MXU_TILE
    per = heads // groups
    vs = []
    for g in range(groups):
        acc = None
        for hd in range(g * per, (g + 1) * per):
            part = _dot(o_ref[:, hd].reshape(rows, kv_rank), wuv_ref[hd])
            acc = part if acc is None else acc + part
        vs.append(acc.astype(BF16))
    v = jnp.concatenate(vs, axis=1)
    x = x_ref[...].reshape(rows, -1)
    y_ref[...] = (x + _dot(v, wo_ref[...])).reshape(y_ref.shape)


def _mla_out(o, x, w, rows=256):
    B, T, D = x.shape
    heads, kv_rank = o.shape[1], o.shape[3]
    bb, tt = _token_tiles(B, T, rows)
    return pl.pallas_call(
        _mla_out_body,
        out_shape=jax.ShapeDtypeStruct(x.shape, F32),
        grid=(B // bb, T // tt),
        in_specs=[pl.BlockSpec((bb, heads, tt, kv_rank), lambda b, t: (b, 0, t, 0)),
                  _tok_spec(bb, tt, D), _full(w["w_uv"].shape), _full(w["w_out"].shape)],
        out_specs=_tok_spec(bb, tt, D),
        compiler_params=_cparams("parallel", "parallel"),
        name="mla_out",
    )(o, x, w["w_uv"], w["w_out"])


def _mlstm_proj_body(x_ref, g_ref, w_ref, q_ref, k_ref, v_ref, o_ref, gt_ref, *, qk, hv, qscale):
    rows = x_ref.shape[0] * x_ref.shape[1]
    x = x_ref[...].reshape(rows, -1)
    h = _rmsnorm(x, g_ref[...]).astype(BF16)
    z = _dot(h, w_ref[...])
    q_ref[...] = (z[:, :qk] * qscale).astype(BF16).reshape(q_ref.shape)
    k_ref[...] = z[:, qk:2 * qk].astype(BF16).reshape(k_ref.shape)
    v_ref[...] = z[:, 2 * qk:2 * qk + hv].astype(BF16).reshape(v_ref.shape)
    o_ref[...] = z[:, 2 * qk + hv:2 * qk + 2 * hv].reshape(o_ref.shape)
    gt_ref[...] = z[:, 2 * qk + 2 * hv:].reshape(gt_ref.shape)


def _mlstm_proj(x, g, w, qk, hv, dk, rows=512):
    B, T, D = x.shape
    bb, tt = _token_tiles(B, T, rows)
    widths = (qk, qk, hv, hv, LANES)
    dtypes = (BF16, BF16, BF16, F32, F32)
    return pl.pallas_call(
        functools.partial(_mlstm_proj_body, qk=qk, hv=hv, qscale=float(dk ** -0.5)),
        out_shape=tuple(jax.ShapeDtypeStruct((B, T, wd), dt) for wd, dt in zip(widths, dtypes)),
        grid=(B // bb, T // tt),
        in_specs=[_tok_spec(bb, tt, D), _full((1, D)), _full(w.shape)],
        out_specs=tuple(_tok_spec(bb, tt, wd) for wd in widths),
        compiler_params=_cparams("parallel", "parallel"),
        name="mlstm_proj",
    )(x, g.reshape(1, D), w)


def _mlstm_body(q_ref, k_ref, v_ref, o_ref, gt_ref, x_ref, c0_ref, n0_ref, m0_ref, bias_ref, gh_ref, wo_ref,
                y_ref, c_ref, n_ref, m_ref, c_scr, n_scr, m_scr, *, heads, dk, dv):
    ci = pl.program_id(1)
    nb, L = q_ref.shape[0], q_ref.shape[1]

    @pl.when(ci == 0)
    def _():
        c_scr[...] = c0_ref[...]
        n_scr[...] = n0_ref[...]
        m_scr[...] = m0_ref[...]

    lane = lax.broadcasted_iota(jnp.int32, (L, LANES), 1)
    row = lax.broadcasted_iota(jnp.int32, (L, L), 0)
    col = lax.broadcasted_iota(jnp.int32, (L, L), 1)
    causal = col <= row
    tri = causal.astype(F32)

    gates, pre = [], {}
    for e in range(nb):
        a = GATE_CAP * jnp.tanh((gt_ref[e] + bias_ref[...]) / GATE_CAP)
        lf = -(jnp.maximum(-a, 0.0) + jnp.log1p(jnp.exp(-jnp.abs(a))))
        xg = jnp.where(lane < heads, a, jnp.where(lane < 2 * heads, lf, 0.0))
        bc = jnp.dot(tri, xg, precision=HIGHEST, preferred_element_type=F32)
        gates.append((xg, bc, xg.T, bc.T))
    for e in range(nb):
        for hd in range(heads):
            qh = q_ref[e, :, hd * dk:(hd + 1) * dk]
            kh = k_ref[e, :, hd * dk:(hd + 1) * dk]
            pre[e, hd] = (_dot_nt(qh, kh), _dot(qh, c_scr[e, hd].astype(BF16)))

    for e in range(nb):
        xg, bc, xg_t, bc_t = gates[e]
        blast_row = bc[L - 1:L, :]
        outs = []
        for hd in range(heads):
            qk_, qc = pre[e, hd]
            b_col = jnp.sum(jnp.where(lane == heads + hd, bc, 0.0), axis=1, keepdims=True)
            ib_row = xg_t[hd:hd + 1, :] - bc_t[heads + hd:heads + hd + 1, :]
            log_d = jnp.where(causal, b_col + ib_row, -jnp.inf)
            i_col = jnp.sum(jnp.where(lane == hd, xg, 0.0), axis=1, keepdims=True)
            b_last = jnp.sum(jnp.where(lane[:1] == heads + hd, blast_row, 0.0), axis=1, keepdims=True)
            m_prev = m_scr[e, hd:hd + 1, :1]
            m_t = jnp.maximum(b_col + m_prev, jnp.max(log_d, axis=1, keepdims=True))
            decay = jnp.exp(b_col + m_prev - m_t)
            qh = q_ref[e, :, hd * dk:(hd + 1) * dk]
            kh = k_ref[e, :, hd * dk:(hd + 1) * dk]
            vh = v_ref[e, :, hd * dv:(hd + 1) * dv]
            c_prev = c_scr[e, hd]
            n_prev = n_scr[e, hd:hd + 1, :]
            sm = qk_ * jnp.exp(log_d - m_t)
            num = _dot(sm.astype(BF16), vh) + decay * qc
            den = (jnp.sum(sm, axis=1, keepdims=True)
                   + decay * jnp.sum(qh.astype(F32) * n_prev, axis=1, keepdims=True))
            hh = num / jnp.maximum(jnp.abs(den), jnp.exp(-m_t))
            hh = (hh * lax.rsqrt(jnp.mean(hh * hh, axis=1, keepdims=True) + NORM_EPS)
                  * gh_ref[:, hd * dv:(hd + 1) * dv])
            outs.append((jax.nn.sigmoid(o_ref[e, :, hd * dv:(hd + 1) * dv]) * hh).astype(BF16))
            m_new = m_t[L - 1:L, :]
            wgt = jnp.exp(i_col + b_last - b_col - m_new)
            cdec = jnp.exp(b_last + m_prev - m_new)
            kw_ = kh.astype(F32) * wgt
            c_scr[e, hd] = cdec * c_prev + _dot_tn(kw_.astype(BF16), vh)
            n_scr[e, hd:hd + 1, :] = cdec * n_prev + jnp.sum(kw_, axis=0, keepdims=True)
            m_scr[e, hd:hd + 1, :] = jnp.broadcast_to(m_new, (1, LANES))
        out = jnp.concatenate(outs, axis=1)
        y_ref[e] = x_ref[e] + _dot(out, wo_ref[...])

    @pl.when(ci == pl.num_programs(1) - 1)
    def _():
        c_ref[...] = c_scr[...]
        n_ref[...] = n_scr[...]
        m_ref[...] = m_scr[...]


def _mlstm(q, k, v, o, gt, x, c0, n0, m0, bias, g_head, w_out, chunk=256, nb=1):
    B, T, D = x.shape
    heads, dk, dv = c0.shape[1], c0.shape[2], c0.shape[3]
    L = min(chunk, T)
    nb = math.gcd(nb, B)
    tok = lambda wd: pl.BlockSpec((nb, L, wd), lambda b, c: (b, c, 0))
    st3 = pl.BlockSpec((nb, heads, LANES), lambda b, c: (b, 0, 0))
    st4 = pl.BlockSpec((nb, heads, dk, dv), lambda b, c: (b, 0, 0, 0))
    return pl.pallas_call(
        functools.partial(_mlstm_body, heads=heads, dk=dk, dv=dv),
        out_shape=(jax.ShapeDtypeStruct(x.shape, F32),
                   jax.ShapeDtypeStruct(c0.shape, F32),
                   jax.ShapeDtypeStruct((B, heads, dk), F32),
                   jax.ShapeDtypeStruct((B, heads, LANES), F32)),
        grid=(B // nb, T // L),
        in_specs=[tok(heads * dk), tok(heads * dk), tok(heads * dv), tok(heads * dv), tok(LANES), tok(D),
                  st4, st3, st3, _full((1, LANES)), _full((1, heads * dv)), _full(w_out.shape)],
        out_specs=(tok(D), st4, st3, st3),
        scratch_shapes=[pltpu.VMEM((nb, heads, dk, dv), F32), pltpu.VMEM((nb, heads, dk), F32),
                        pltpu.VMEM((nb, heads, LANES), F32)],
        compiler_params=_cparams("parallel", "arbitrary"),
        name="mlstm",
    )(q, k, v, o, gt, x, c0, n0, m0, bias, g_head.reshape(1, heads * dv), w_out)


def kernel(x_prompt, x_sample, cache_ckv, cache_krope, state_C, state_n, state_m, page_table, p_prompt, p_sample, g_ffa, w_ffa_in, w_ffa_out, g_mix, g_ffb, w_ffb_in, w_ffb_out, g_ple, w_ple_gate, w_ple_proj, w_mla_in, g_q_lat, g_kv_lat, w_uq, w_uk, w_uv, w_mla_out, w_mlstm_in, b_igate, b_fgate, g_mlstm_head, w_mlstm_out, g_final):
    depth = g_ffa.shape[0]
    seq, dec_seq = x_prompt.shape[1], x_sample.shape[1]
    past = page_table.shape[1] * cache_ckv.shape[2]
    positions = (jnp.arange(seq), past + jnp.arange(dec_seq))
    xs = [x_prompt, x_sample]
    ps = (p_prompt, p_sample)
    heads_m, dk, dv = state_C.shape[2], state_C.shape[3], state_C.shape[4]
    assert dk == LANES and 2 * heads_m <= LANES
    qk, hv = heads_m * dk, heads_m * dv
    outs = {name: ([], []) for name in ("ckv", "kr", "C", "n", "m")}
    krope_t = jnp.swapaxes(cache_krope, 2, 3)

    for i in range(depth):
        j = i // 2
        bf = lambda a: a.astype(BF16)
        w_ffa = (bf(w_ffa_in[i]), bf(w_ffa_out[i]))
        w_ffb = (bf(w_ffb_in[i]), bf(w_ffb_out[i]))
        w_gate, w_proj = bf(w_ple_gate[i]), bf(w_ple_proj[i])
        if i % 2 == 0:
            mla = _prep_mla(w_mla_in[j], g_q_lat[j], g_kv_lat[j], w_uq[j], w_uk[j], w_uv[j], w_mla_out[j])
        else:
            wi = w_mlstm_in[j]
            pad = LANES - 2 * heads_m
            w_ml = bf(jnp.pad(wi, ((0, 0), (0, pad))))
            bias = jnp.pad(jnp.concatenate([b_igate[j], b_fgate[j]]), (0, pad)).reshape(1, LANES)
            w_mo = bf(w_mlstm_out[j])
        for g in range(2):
            x = xs[g]
            nb = x.shape[0]
            x = _ffn(x, g_ffa[i], *w_ffa)
            if i % 2 == 0:
                tab = _rope_table(positions[g], mla["rope"])
                q, k, ckv, kr = _mla_proj(x, g_mix[i], mla, tab)
                if g == 0:
                    o = _attn_prompt(q, k, mla["kv_rank"])
                else:
                    o = _attn_decode(q, k, cache_ckv, krope_t, j, page_table, mla["kv_rank"], mla["rope"])
                x = _mla_out(o, x, mla)
                outs["ckv"][g].append(ckv)
                outs["kr"][g].append(kr)
            else:
                if g == 0:
                    c0 = jnp.zeros((nb, heads_m, dk, dv), F32)
                    n0 = jnp.zeros((nb, heads_m, dk), F32)
                    m0 = jnp.zeros((nb, heads_m), F32)
                else:
                    c0, n0, m0 = state_C[j], state_n[j], state_m[j]
                m0 = jnp.broadcast_to(m0[:, :, None], (nb, heads_m, LANES))
                qm, km, vm, om, gt = _mlstm_proj(x, g_mix[i], w_ml, qk, hv, dk)
                x, c_new, n_new, m_new = _mlstm(qm, km, vm, om, gt, x, c0, n0, m0, bias, g_mlstm_head[j], w_mo,
                                                nb=1 if g == 0 else 4)
                outs["C"][g].append(c_new)
                outs["n"][g].append(n_new)
                outs["m"][g].append(m_new[:, :, 0])
            x = _ffn(x, g_ffb[i], *w_ffb)
            x = _ple(x, ps[g][i], g_ple[i], w_gate, w_proj, g_final, final=(i == depth - 1))
            xs[g] = x

    st = lambda name, g: jnp.stack(outs[name][g])
    return (xs[0], xs[1], st("ckv", 0), st("kr", 0), st("ckv", 1), st("kr", 1),
            st("C", 0), st("n", 0), st("m", 0), st("C", 1), st("n", 1), st("m", 1))
```

```python
import functools
import math

import jax
import jax.numpy as jnp
import numpy as np
from jax import lax
from jax.experimental import pallas as pl
from jax.experimental.pallas import tpu as pltpu

NORM_EPS = 1e-6
ROPE_BASE = 10000.0
GATE_CAP = 15.0

LANES = 128
MXU_TILE = 256
VMEM_LIMIT = 56 * 1024 * 1024

F32 = jnp.float32
BF16 = jnp.bfloat16
HIGHEST = lax.Precision.HIGHEST


def _cparams(*sem):
    return pltpu.CompilerParams(dimension_semantics=sem, vmem_limit_bytes=VMEM_LIMIT)


def _dot(a, b):
    return jnp.dot(a, b, preferred_element_type=F32)


def _dot_nt(a, b, precision=None):
    return lax.dot_general(a, b, (((1,), (1,)), ((), ())), precision=precision,
                           preferred_element_type=F32)


def _dot_tn(a, b):
    return lax.dot_general(a, b, (((0,), (0,)), ((), ())), preferred_element_type=F32)


def _rmsnorm(x, g):
    return x * lax.rsqrt(jnp.mean(x * x, axis=-1, keepdims=True) + NORM_EPS) * g


def _full(shape):
    zeros = (0,) * len(shape)
    return pl.BlockSpec(shape, lambda *_: zeros)


def _tok_spec(bb, tt, width):
    return pl.BlockSpec((bb, tt, width), lambda b, t, *_: (b, t, 0))


def _token_tiles(B, T, rows):
    if T >= rows:
        return 1, rows
    return min(B, rows // T), T


def _ffn_body(x_ref, g_ref, win_ref, wo_ref, o_ref, *, sub):
    bb, tt, d = x_ref.shape
    f = wo_ref.shape[0]
    nb = sub // tt if bb > 1 else 1
    for r in range(bb * tt // sub):
        if bb == 1:
            x = x_ref[0, r * sub:(r + 1) * sub, :]
        else:
            x = x_ref[r * nb:(r + 1) * nb].reshape(sub, d)
        h = _rmsnorm(x, g_ref[...]).astype(BF16)
        a = _dot(h, win_ref[:, :f])
        b = _dot(h, win_ref[:, f:])
        act = (a * jax.nn.sigmoid(a) * b).astype(BF16)
        y = x + 0.5 * _dot(act, wo_ref[...])
        if bb == 1:
            o_ref[0, r * sub:(r + 1) * sub, :] = y
        else:
            o_ref[r * nb:(r + 1) * nb] = y.reshape(nb, tt, d)


def _ffn(x, g, w_in, w_out, rows=512, sub=256):
    B, T, D = x.shape
    F = w_out.shape[0]
    bb, tt = _token_tiles(B, T, rows)
    sub = min(sub, bb * tt)
    resident = pl.Buffered(1)
    return pl.pallas_call(
        functools.partial(_ffn_body, sub=sub),
        out_shape=jax.ShapeDtypeStruct(x.shape, F32),
        grid=(B // bb, T // tt),
        in_specs=[
            _tok_spec(bb, tt, D),
            _full((1, D)),
            pl.BlockSpec((D, 2 * F), lambda b, t: (0, 0), pipeline_mode=resident),
            pl.BlockSpec((F, D), lambda b, t: (0, 0), pipeline_mode=resident),
        ],
        out_specs=_tok_spec(bb, tt, D),
        compiler_params=_cparams("parallel", "parallel"),
        name="ffn",
    )(x, g.reshape(1, D), w_in, w_out)


def _ple_body(x_ref, p_ref, g_ref, wg_ref, wp_ref, gf_ref, o_ref, *, final):
    rows = x_ref.shape[0] * x_ref.shape[1]
    x = x_ref[...].reshape(rows, -1)
    p = p_ref[...].reshape(rows, -1).astype(BF16)
    hn = _rmsnorm(x, g_ref[...]).astype(BF16)
    gate = jax.nn.sigmoid(_dot(hn, wg_ref[...]))
    y = x + gate * _dot(p, wp_ref[...])
    if final:
        y = _rmsnorm(y, gf_ref[...])
    o_ref[...] = y.reshape(o_ref.shape)


def _ple(x, p, g, w_gate, w_proj, g_final, final, rows=512):
    B, T, D = x.shape
    P = p.shape[-1]
    bb, tt = _token_tiles(B, T, rows)
    return pl.pallas_call(
        functools.partial(_ple_body, final=final),
        out_shape=jax.ShapeDtypeStruct(x.shape, F32),
        grid=(B // bb, T // tt),
        in_specs=[_tok_spec(bb, tt, D), _tok_spec(bb, tt, P), _full((1, D)),
                  _full((D, D)), _full((P, D)), _full((1, D))],
        out_specs=_tok_spec(bb, tt, D),
        compiler_params=_cparams("parallel", "parallel"),
        name="ple",
    )(x, p, g.reshape(1, D), w_gate, w_proj, g_final.reshape(1, D))


def _mla_proj_body(x_ref, gm_ref, win_ref, gq_ref, gkv_ref, wqn_ref, wqr_ref, wuk_ref, tab_ref,
                   q_ref, k_ref, ckv_ref, kr_ref, *, scale, q_rank, kv_rank, rope):
    bb, tt = x_ref.shape[0], x_ref.shape[1]
    rows = bb * tt
    heads = q_ref.shape[1]
    x = x_ref[...].reshape(rows, -1)
    h = _rmsnorm(x, gm_ref[...]).astype(BF16)
    z = _dot(h, win_ref[...])
    cq = _rmsnorm(z[:, :q_rank], gq_ref[...]).astype(BF16)
    ckv = _rmsnorm(z[:, q_rank:q_rank + kv_rank], gkv_ref[...])
    tab = jnp.broadcast_to(tab_ref[...][None], (bb, tt, LANES)).reshape(rows, LANES)
    yk = z[:, q_rank + kv_rank:] * tab
    kr = yk + pltpu.roll(yk, LANES - rope, 1)
    lane = lax.broadcasted_iota(jnp.int32, (rows, LANES), 1)
    kdup = jnp.where(lane < rope, kr, jnp.where(lane < 2 * rope, pltpu.roll(kr, rope, 1), 0.0))
    ckv_ref[...] = ckv.reshape(ckv_ref.shape)
    kr_ref[...] = kr[:, :rope].reshape(kr_ref.shape)
    k_ref[:, :, :kv_rank] = ckv.astype(BF16).reshape(bb, tt, kv_rank)
    k_ref[:, :, kv_rank:] = kdup.astype(BF16).reshape(bb, tt, LANES)

    qn = _dot(cq, wqn_ref[...]).astype(BF16)
    per = heads // (qn.shape[1] // MXU_TILE)
    for hd in range(heads):
        c0 = (hd // per) * MXU_TILE
        qa = _dot(qn[:, c0:c0 + MXU_TILE], wuk_ref[hd]) * scale
        qy = _dot(cq, wqr_ref[hd]) * tab * scale
        q_ref[:, hd, :, :kv_rank] = qa.astype(BF16).reshape(bb, tt, kv_rank)
        q_ref[:, hd, :, kv_rank:] = qy.astype(BF16).reshape(bb, tt, LANES)


def _mla_proj(x, g_mix, w, tab, rows=256):
    B, T, D = x.shape
    heads, q_rank, kv_rank, rope = w["heads"], w["q_rank"], w["kv_rank"], w["rope"]
    bb, tt = _token_tiles(B, T, rows)
    kw = kv_rank + LANES
    body = functools.partial(_mla_proj_body, scale=w["scale"], q_rank=q_rank, kv_rank=kv_rank, rope=rope)
    return pl.pallas_call(
        body,
        out_shape=(
            jax.ShapeDtypeStruct((B, heads, T, kw), BF16),
            jax.ShapeDtypeStruct((B, T, kw), BF16),
            jax.ShapeDtypeStruct((B, T, kv_rank), F32),
            jax.ShapeDtypeStruct((B, T, rope), F32),
        ),
        grid=(B // bb, T // tt),
        in_specs=[
            _tok_spec(bb, tt, D), _full((1, D)), _full(w["w_in"].shape), _full((1, q_rank)),
            _full((1, kv_rank)), _full(w["w_qn"].shape), _full(w["w_qr"].shape), _full(w["w_uk"].shape),
            pl.BlockSpec((tt, LANES), lambda b, t: (t, 0)),
        ],
        out_specs=(
            pl.BlockSpec((bb, heads, tt, kw), lambda b, t: (b, 0, t, 0)),
            _tok_spec(bb, tt, kw), _tok_spec(bb, tt, kv_rank), _tok_spec(bb, tt, rope),
        ),
        compiler_params=_cparams("parallel", "parallel"),
        name="mla_proj",
    )(x, g_mix.reshape(1, D), w["w_in"], w["g_q"].reshape(1, q_rank), w["g_kv"].reshape(1, kv_rank),
      w["w_qn"], w["w_qr"], w["w_uk"], tab)


def _prep_mla(w_mla_in, g_q, g_kv, w_uq, w_uk, w_uv, w_out):
    q_rank, heads, qd = w_uq.shape
    kv_rank, _, nope = w_uk.shape
    vdim = w_uv.shape[2]
    rope = qd - nope
    half = rope // 2
    assert kv_rank == MXU_TILE and MXU_TILE % nope == 0 and MXU_TILE % vdim == 0 and 2 * rope <= LANES
    d = w_mla_in.shape[0]
    wr = w_mla_in[:, q_rank + kv_rank:]
    w_in = jnp.concatenate(
        [w_mla_in, wr[:, half:], wr[:, :half], jnp.zeros((d, LANES - 2 * rope), F32)], axis=1).astype(BF16)
    w_qn = w_uq[:, :, :nope].reshape(q_rank, heads * nope).astype(BF16)
    x1, x2 = w_uq[:, :, nope:nope + half], w_uq[:, :, nope + half:]
    w_qr = jnp.concatenate([x1, x2, x2, x1, jnp.zeros((q_rank, heads, LANES - 2 * rope), F32)], axis=2)
    w_qr = jnp.transpose(w_qr, (1, 0, 2)).astype(BF16)
    per = MXU_TILE // nope
    slot = jax.nn.one_hot(jnp.arange(heads) % per, per, dtype=F32)
    w_ukp = jnp.einsum("rhd,hs->hsdr", w_uk, slot).reshape(heads, MXU_TILE, kv_rank).astype(BF16)
    perv = MXU_TILE // vdim
    slotv = jax.nn.one_hot(jnp.arange(heads) % perv, perv, dtype=F32)
    w_uvp = jnp.einsum("rhv,hs->hrsv", w_uv, slotv).reshape(heads, kv_rank, MXU_TILE).astype(BF16)
    return dict(heads=heads, q_rank=q_rank, kv_rank=kv_rank, rope=rope, nope=nope, vdim=vdim,
                scale=float((nope + rope) ** -0.5 * math.log2(math.e)), w_in=w_in, g_q=g_q, g_kv=g_kv, w_qn=w_qn,
                w_qr=w_qr, w_uk=w_ukp, w_uv=w_uvp, w_out=w_out.astype(BF16))


def _rope_table(pos, rope):
    half = rope // 2
    inv = ROPE_BASE ** (-jnp.arange(half, dtype=F32) / half)
    ang = pos.astype(F32)[:, None] * inv[None, :]
    c, s = jnp.cos(ang), jnp.sin(ang)
    return jnp.concatenate([c, c, -s, s, jnp.zeros((pos.shape[0], LANES - 2 * rope), F32)], axis=1)


def _lane_tile(stat, width):
    if width % LANES:
        return stat[:, :1]
    return stat if width == LANES else jnp.tile(stat, (1, width // LANES))


def _softmax_step(s, v, m_scr, l_scr, acc_scr):
    m_prev = m_scr[...]
    m_new = jnp.maximum(m_prev, jnp.max(s, axis=1, keepdims=True))
    alpha = jnp.exp2(m_prev - m_new)
    p = jnp.exp2(s - _lane_tile(m_new, s.shape[1]))
    l_scr[...] = alpha * l_scr[...] + jnp.sum(p, axis=1, keepdims=True)
    acc_scr[...] = acc_scr[...] * _lane_tile(alpha, v.shape[1]) + _dot(p.astype(BF16), v)
    m_scr[...] = m_new


def _softmax_result(l_scr, acc_scr):
    return acc_scr[...] / _lane_tile(l_scr[...], acc_scr.shape[-1])


def _softmax_init(m_scr, l_scr, acc_scr):
    m_scr[...] = jnp.full_like(m_scr, -jnp.inf)
    l_scr[...] = jnp.zeros_like(l_scr)
    acc_scr[...] = jnp.zeros_like(acc_scr)


def _attn_prompt_body(qi_ref, kj_ref, q_ref, k_ref, x_ref, wuv_ref, wo_ref, y_ref, m_scr, l_scr, acc_scr,
                      *, tq, tk, kv_rank, ahead):
    s_id = pl.program_id(1)
    qi, kj = qi_ref[s_id], kj_ref[s_id]
    heads = q_ref.shape[1]
    groups, rows = m_scr.shape[0], m_scr.shape[1]
    hb = heads // groups

    @pl.when(kj == 0)
    def _():
        _softmax_init(m_scr, l_scr, acc_scr)

    k = k_ref[0]
    v = k[:, :kv_rank]
    qpos = qi * tq + lax.broadcasted_iota(jnp.int32, (tq, tk), 0)
    kpos = kj * tk + lax.broadcasted_iota(jnp.int32, (tq, tk), 1)
    bias = jnp.where(kpos <= qpos, 0.0, -jnp.inf)

    def scores(g):
        q = q_ref[0, g * hb:(g + 1) * hb].reshape(rows, -1)
        return (_dot_nt(q, k).reshape(hb, tq, tk) + bias[None]).reshape(rows, tk)

    pending = [scores(g) for g in range(min(ahead, groups))]
    for g in range(groups):
        if g + ahead < groups:
            pending.append(scores(g + ahead))
        _softmax_step(pending.pop(0), v, m_scr.at[g], l_scr.at[g], acc_scr.at[g])

    @pl.when(kj == ((qi + 1) * tq - 1) // tk)
    def _():
        def latent_out(hd):
            g, r = divmod(hd, hb)
            o = _softmax_result(l_scr.at[g], acc_scr.at[g])[r * tq:(r + 1) * tq]
            return o.astype(BF16)

        y_ref[0] = _mla_project_out(latent_out, heads, x_ref[0], wuv_ref, wo_ref)


def _mla_project_out(latent_out, heads, x, wuv_ref, wo_ref):
    groups = wo_ref.shape[0] // MXU_TILE
    per = heads // groups
    vs = []
    for g in range(groups):
        acc = None
        for hd in range(g * per, (g + 1) * per):
            part = _dot(latent_out(hd), wuv_ref[hd])
            acc = part if acc is None else acc + part
        vs.append(acc.astype(BF16))
    v = jnp.concatenate(vs, axis=1)
    return x + _dot(v, wo_ref[...])


def _attn_prompt(q, k, x, w, tq=512, tk=512, hb=1, ahead=1):
    kv_rank = w["kv_rank"]
    D = x.shape[-1]
    resident = pl.Buffered(1)
    B, heads, T, kw = q.shape
    tq, tk = min(tq, T), min(tk, T)
    groups, rows = heads // hb, hb * tq
    pairs = [(i, j) for i in range(T // tq) for j in range(((i + 1) * tq - 1) // tk + 1)]
    qi = jnp.asarray(np.array([p[0] for p in pairs], np.int32))
    kj = jnp.asarray(np.array([p[1] for p in pairs], np.int32))
    grid_spec = pltpu.PrefetchScalarGridSpec(
        num_scalar_prefetch=2,
        grid=(B, len(pairs)),
        in_specs=[
            pl.BlockSpec((1, heads, tq, kw), lambda b, s, qi, kj: (b, 0, qi[s], 0)),
            pl.BlockSpec((1, tk, kw), lambda b, s, qi, kj: (b, kj[s], 0)),
            pl.BlockSpec((1, tq, D), lambda b, s, qi, kj: (b, qi[s], 0)),
            pl.BlockSpec(w["w_uv"].shape, lambda b, s, qi, kj: (0, 0, 0), pipeline_mode=resident),
            pl.BlockSpec(w["w_out"].shape, lambda b, s, qi, kj: (0, 0), pipeline_mode=resident),
        ],
        out_specs=pl.BlockSpec((1, tq, D), lambda b, s, qi, kj: (b, qi[s], 0)),
        scratch_shapes=[pltpu.VMEM((groups, rows, LANES), F32), pltpu.VMEM((groups, rows, LANES), F32),
                        pltpu.VMEM((groups, rows, kv_rank), F32)],
    )
    return pl.pallas_call(
        functools.partial(_attn_prompt_body, tq=tq, tk=tk, kv_rank=kv_rank, ahead=ahead),
        out_shape=jax.ShapeDtypeStruct(x.shape, F32),
        grid_spec=grid_spec,
        compiler_params=_cparams("parallel", "arbitrary"),
        name="attn_prompt",
    )(qi, kj, q, k, x, w["w_uv"], w["w_out"])


def _attn_decode_body(pt_ref, q_ref, kn_ref, ck_hbm, kr_hbm, o_ref, ck_buf, kr_buf, sem, m_scr, l_scr, acc_scr,
                      *, layer, kv_rank, rope):
    step, nsteps = pl.program_id(0), pl.num_programs(0)
    nb, heads, tn = q_ref.shape[0], q_ref.shape[1], q_ref.shape[2]
    pages, page = ck_buf.shape[2], ck_buf.shape[3]
    n_chunks = pt_ref.shape[1] // pages
    rows = heads * tn

    def page_copies(slot, page_id):
        copies = []
        for e in range(nb):
            for j in range(pages):
                pid = page_id(e, j)
                copies.append(pltpu.make_async_copy(ck_hbm.at[layer, pid], ck_buf.at[slot, e, j], sem.at[slot]))
                copies.append(pltpu.make_async_copy(kr_hbm.at[layer, pid], kr_buf.at[slot, e, j], sem.at[slot]))
        return copies

    def start_chunk(slot, at_step, chunk):
        for c in page_copies(slot, lambda e, j: pt_ref[at_step * nb + e, chunk * pages + j]):
            c.start()

    def wait_chunk(slot):
        for c in page_copies(slot, lambda e, j: 0):
            c.wait()

    @pl.when(step == 0)
    def _():
        start_chunk(0, 0, 0)

    _softmax_init(m_scr, l_scr, acc_scr)

    def queries(e):
        q = q_ref[e].reshape(rows, -1)
        qy = q[:, kv_rank:].astype(F32)
        qr = (qy + pltpu.roll(qy, LANES - rope, 1))[:, :rope].astype(BF16)
        return q, q[:, :kv_rank], qr

    def attend(slot):
        scores = []
        for e in range(nb):
            _, qa, qr = queries(e)
            kc = ck_buf[slot, e].reshape(pages * page, kv_rank).astype(BF16)
            krt = jnp.concatenate([kr_buf[slot, e, j].astype(BF16) for j in range(pages)], axis=1)
            scores.append((_dot_nt(qa, kc) + _dot(qr, krt), kc))
        for e, (s, kc) in enumerate(scores):
            _softmax_step(s, kc, m_scr.at[e], l_scr.at[e], acc_scr.at[e])

    def chunk_pair(c2, carry):
        start_chunk(1, step, 2 * c2 + 1)
        wait_chunk(0)
        attend(0)
        last = c2 == n_chunks // 2 - 1
        nxt_step = jnp.where(last, step + 1, step)
        nxt_chunk = jnp.where(last, 0, 2 * c2 + 2)

        @pl.when(nxt_step < nsteps)
        def _():
            start_chunk(0, nxt_step, nxt_chunk)

        wait_chunk(1)
        attend(1)
        return carry

    lax.fori_loop(0, n_chunks // 2, chunk_pair, 0)

    for e in range(nb):
        q, _, _ = queries(e)
        kn = kn_ref[e]
        s = _dot_nt(q, kn)
        tq_ = lax.broadcasted_iota(jnp.int32, (heads, tn, tn), 1).reshape(rows, tn)
        tk_ = lax.broadcasted_iota(jnp.int32, (rows, tn), 1)
        s = jnp.where(tk_ <= tq_, s, -jnp.inf)
        _softmax_step(s, kn[:, :kv_rank], m_scr.at[e], l_scr.at[e], acc_scr.at[e])
        o = _softmax_result(l_scr.at[e], acc_scr.at[e])
        o_ref[e] = o.astype(BF16).reshape(heads, tn, kv_rank)


def _attn_decode(q, k_new, cache_ckv, cache_krope_t, layer, page_table, kv_rank, rope, pages=16, nb=4):
    B, heads, tn, kw = q.shape
    n_pages = page_table.shape[1]
    page = cache_ckv.shape[2]
    nb = math.gcd(nb, B)
    pages = math.gcd(pages, n_pages // 2)
    assert (n_pages // pages) % 2 == 0, "the two-slot ring walks chunks in pairs"
    rows = heads * tn
    grid_spec = pltpu.PrefetchScalarGridSpec(
        num_scalar_prefetch=1,
        grid=(B // nb,),
        in_specs=[pl.BlockSpec((nb, heads, tn, kw), lambda b, pt: (b, 0, 0, 0)),
                  pl.BlockSpec((nb, tn, kw), lambda b, pt: (b, 0, 0)),
                  pl.BlockSpec(memory_space=pl.ANY),
                  pl.BlockSpec(memory_space=pl.ANY)],
        out_specs=pl.BlockSpec((nb, heads, tn, kv_rank), lambda b, pt: (b, 0, 0, 0)),
        scratch_shapes=[pltpu.VMEM((2, nb, pages, page, kv_rank), cache_ckv.dtype),
                        pltpu.VMEM((2, nb, pages, rope, page), cache_krope_t.dtype),
                        pltpu.SemaphoreType.DMA((2,)),
                        pltpu.VMEM((nb, rows, LANES), F32), pltpu.VMEM((nb, rows, LANES), F32),
                        pltpu.VMEM((nb, rows, kv_rank), F32)],
    )
    return pl.pallas_call(
        functools.partial(_attn_decode_body, layer=layer, kv_rank=kv_rank, rope=rope),
        out_shape=jax.ShapeDtypeStruct((B, heads, tn, kv_rank), BF16),
        grid_spec=grid_spec,
        compiler_params=_cparams("arbitrary"),
        name="attn_decode",
    )(page_table, q, k_new, cache_ckv, cache_krope_t)


def _mla_out_body(o_ref, x_ref, wuv_ref, wo_ref, y_ref):
    bb, heads, tt, kv_rank = o_ref.shape
    rows = bb * tt
    x = x_ref[...].reshape(rows, -1)
    y = _mla_project_out(lambda hd: o_ref[:, hd].reshape(rows, kv_rank), heads, x, wuv_ref, wo_ref)
    y_ref[...] = y.reshape(y_ref.shape)


def _mla_out(o, x, w, rows=256):
    B, T, D = x.shape
    heads, kv_rank = o.shape[1], o.shape[3]
    bb, tt = _token_tiles(B, T, rows)
    return pl.pallas_call(
        _mla_out_body,
        out_shape=jax.ShapeDtypeStruct(x.shape, F32),
        grid=(B // bb, T // tt),
        in_specs=[pl.BlockSpec((bb, heads, tt, kv_rank), lambda b, t: (b, 0, t, 0)),
                  _tok_spec(bb, tt, D), _full(w["w_uv"].shape), _full(w["w_out"].shape)],
        out_specs=_tok_spec(bb, tt, D),
        compiler_params=_cparams("parallel", "parallel"),
        name="mla_out",
    )(o, x, w["w_uv"], w["w_out"])


def _mlstm_proj_body(x_ref, g_ref, w_ref, q_ref, k_ref, v_ref, o_ref, gt_ref, *, qk, hv, qscale):
    rows = x_ref.shape[0] * x_ref.shape[1]
    x = x_ref[...].reshape(rows, -1)
    h = _rmsnorm(x, g_ref[...]).astype(BF16)
    z = _dot(h, w_ref[...])
    q_ref[...] = (z[:, :qk] * qscale).astype(BF16).reshape(q_ref.shape)
    k_ref[...] = z[:, qk:2 * qk].astype(BF16).reshape(k_ref.shape)
    v_ref[...] = z[:, 2 * qk:2 * qk + hv].astype(BF16).reshape(v_ref.shape)
    o_ref[...] = z[:, 2 * qk + hv:2 * qk + 2 * hv].reshape(o_ref.shape)
    gt_ref[...] = z[:, 2 * qk + 2 * hv:].reshape(gt_ref.shape)


def _mlstm_proj(x, g, w, qk, hv, dk, rows=512):
    B, T, D = x.shape
    bb, tt = _token_tiles(B, T, rows)
    widths = (qk, qk, hv, hv, LANES)
    dtypes = (BF16, BF16, BF16, F32, F32)
    return pl.pallas_call(
        functools.partial(_mlstm_proj_body, qk=qk, hv=hv, qscale=float(dk ** -0.5)),
        out_shape=tuple(jax.ShapeDtypeStruct((B, T, wd), dt) for wd, dt in zip(widths, dtypes)),
        grid=(B // bb, T // tt),
        in_specs=[_tok_spec(bb, tt, D), _full((1, D)), _full(w.shape)],
        out_specs=tuple(_tok_spec(bb, tt, wd) for wd in widths),
        compiler_params=_cparams("parallel", "parallel"),
        name="mlstm_proj",
    )(x, g.reshape(1, D), w)


def _mlstm_body(q_ref, k_ref, v_ref, o_ref, gt_ref, x_ref, c0_ref, n0_ref, m0_ref, bias_ref, gh_ref, wo_ref,
                y_ref, c_ref, n_ref, m_ref, c_scr, n_scr, m_scr, *, heads, dk, dv):
    ci = pl.program_id(1)
    nb, L = q_ref.shape[0], q_ref.shape[1]

    @pl.when(ci == 0)
    def _():
        c_scr[...] = c0_ref[...]
        n_scr[...] = n0_ref[...]
        m_scr[...] = m0_ref[...]

    lane = lax.broadcasted_iota(jnp.int32, (L, LANES), 1)
    row = lax.broadcasted_iota(jnp.int32, (L, L), 0)
    col = lax.broadcasted_iota(jnp.int32, (L, L), 1)
    causal = col <= row
    tri = causal.astype(F32)

    gates, pre = [], {}
    for e in range(nb):
        a = GATE_CAP * jnp.tanh((gt_ref[e] + bias_ref[...]) / GATE_CAP)
        lf = -(jnp.maximum(-a, 0.0) + jnp.log1p(jnp.exp(-jnp.abs(a))))
        xg = jnp.where(lane < heads, a, jnp.where(lane < 2 * heads, lf, 0.0))
        bc = jnp.dot(tri, xg, precision=HIGHEST, preferred_element_type=F32)
        gates.append((xg, bc, xg.T, bc.T))
    for e in range(nb):
        for hd in range(heads):
            qh = q_ref[e, :, hd * dk:(hd + 1) * dk]
            kh = k_ref[e, :, hd * dk:(hd + 1) * dk]
            pre[e, hd] = (_dot_nt(qh, kh), _dot(qh, c_scr[e, hd].astype(BF16)))

    for e in range(nb):
        xg, bc, xg_t, bc_t = gates[e]
        blast_row = bc[L - 1:L, :]
        outs, post = [], []
        for hd in range(heads):
            qk_, qc = pre[e, hd]
            b_col = jnp.sum(jnp.where(lane == heads + hd, bc, 0.0), axis=1, keepdims=True)
            ib_row = xg_t[hd:hd + 1, :] - bc_t[heads + hd:heads + hd + 1, :]
            log_d = jnp.where(causal, b_col + ib_row, -jnp.inf)
            i_col = jnp.sum(jnp.where(lane == hd, xg, 0.0), axis=1, keepdims=True)
            b_last = jnp.sum(jnp.where(lane[:1] == heads + hd, blast_row, 0.0), axis=1, keepdims=True)
            m_prev = m_scr[e, hd:hd + 1, :1]
            m_t = jnp.maximum(b_col + m_prev, jnp.max(log_d, axis=1, keepdims=True))
            decay = jnp.exp(b_col + m_prev - m_t)
            qh = q_ref[e, :, hd * dk:(hd + 1) * dk]
            kh = k_ref[e, :, hd * dk:(hd + 1) * dk]
            vh = v_ref[e, :, hd * dv:(hd + 1) * dv]
            c_prev = c_scr[e, hd]
            n_prev = n_scr[e, hd:hd + 1, :]
            sm = qk_ * jnp.exp(log_d - m_t)
            num = _dot(sm.astype(BF16), vh) + decay * qc
            den = (jnp.sum(sm, axis=1, keepdims=True)
                   + decay * jnp.sum(qh.astype(F32) * n_prev, axis=1, keepdims=True))
            hh = num / jnp.maximum(jnp.abs(den), jnp.exp(-m_t))
            hh = (hh * lax.rsqrt(jnp.mean(hh * hh, axis=1, keepdims=True) + NORM_EPS)
                  * gh_ref[:, hd * dv:(hd + 1) * dv])
            outs.append((jax.nn.sigmoid(o_ref[e, :, hd * dv:(hd + 1) * dv]) * hh).astype(BF16))
            post.append((hd, m_t, m_prev, i_col, b_col, b_last, kh, vh, c_prev, n_prev))
        for hd, m_t, m_prev, i_col, b_col, b_last, kh, vh, c_prev, n_prev in post:
            m_new = m_t[L - 1:L, :]
            wgt = jnp.exp(i_col + b_last - b_col - m_new)
            cdec = jnp.exp(b_last + m_prev - m_new)
            kw_ = kh.astype(F32) * wgt
            c_scr[e, hd] = cdec * c_prev + _dot_tn(kw_.astype(BF16), vh)
            n_scr[e, hd:hd + 1, :] = cdec * n_prev + jnp.sum(kw_, axis=0, keepdims=True)
            m_scr[e, hd:hd + 1, :] = jnp.broadcast_to(m_new, (1, LANES))
        out = jnp.concatenate(outs, axis=1)
        y_ref[e] = x_ref[e] + _dot(out, wo_ref[...])

    @pl.when(ci == pl.num_programs(1) - 1)
    def _():
        c_ref[...] = c_scr[...]
        n_ref[...] = n_scr[...]
        m_ref[...] = m_scr[...]


def _mlstm(q, k, v, o, gt, x, c0, n0, m0, bias, g_head, w_out, chunk=256, nb=1):
    B, T, D = x.shape
    heads, dk, dv = c0.shape[1], c0.shape[2], c0.shape[3]
    L = min(chunk, T)
    nb = math.gcd(nb, B)
    tok = lambda wd: pl.BlockSpec((nb, L, wd), lambda b, c: (b, c, 0))
    st3 = pl.BlockSpec((nb, heads, LANES), lambda b, c: (b, 0, 0))
    st4 = pl.BlockSpec((nb, heads, dk, dv), lambda b, c: (b, 0, 0, 0))
    return pl.pallas_call(
        functools.partial(_mlstm_body, heads=heads, dk=dk, dv=dv),
        out_shape=(jax.ShapeDtypeStruct(x.shape, F32),
                   jax.ShapeDtypeStruct(c0.shape, F32),
                   jax.ShapeDtypeStruct((B, heads, dk), F32),
                   jax.ShapeDtypeStruct((B, heads, LANES), F32)),
        grid=(B // nb, T // L),
        in_specs=[tok(heads * dk), tok(heads * dk), tok(heads * dv), tok(heads * dv), tok(LANES), tok(D),
                  st4, st3, st3, _full((1, LANES)), _full((1, heads * dv)), _full(w_out.shape)],
        out_specs=(tok(D), st4, st3, st3),
        scratch_shapes=[pltpu.VMEM((nb, heads, dk, dv), F32), pltpu.VMEM((nb, heads, dk), F32),
                        pltpu.VMEM((nb, heads, LANES), F32)],
        compiler_params=_cparams("parallel", "arbitrary"),
        name="mlstm",
    )(q, k, v, o, gt, x, c0, n0, m0, bias, g_head.reshape(1, heads * dv), w_out)


def kernel(x_prompt, x_sample, cache_ckv, cache_krope, state_C, state_n, state_m, page_table, p_prompt, p_sample, g_ffa, w_ffa_in, w_ffa_out, g_mix, g_ffb, w_ffb_in, w_ffb_out, g_ple, w_ple_gate, w_ple_proj, w_mla_in, g_q_lat, g_kv_lat, w_uq, w_uk, w_uv, w_mla_out, w_mlstm_in, b_igate, b_fgate, g_mlstm_head, w_mlstm_out, g_final):
    depth = g_ffa.shape[0]
    seq, dec_seq = x_prompt.shape[1], x_sample.shape[1]
    past = page_table.shape[1] * cache_ckv.shape[2]
    positions = (jnp.arange(seq), past + jnp.arange(dec_seq))
    xs = [x_prompt, x_sample]
    ps = (p_prompt, p_sample)
    heads_m, dk, dv = state_C.shape[2], state_C.shape[3], state_C.shape[4]
    assert dk == LANES and 2 * heads_m <= LANES
    qk, hv = heads_m * dk, heads_m * dv
    outs = {name: ([], []) for name in ("ckv", "kr", "C", "n", "m")}
    krope_t = jnp.swapaxes(cache_krope, 2, 3)

    for i in range(depth):
        j = i // 2
        bf = lambda a: a.astype(BF16)
        w_ffa = (bf(w_ffa_in[i]), bf(w_ffa_out[i]))
        w_ffb = (bf(w_ffb_in[i]), bf(w_ffb_out[i]))
        w_gate, w_proj = bf(w_ple_gate[i]), bf(w_ple_proj[i])
        if i % 2 == 0:
            mla = _prep_mla(w_mla_in[j], g_q_lat[j], g_kv_lat[j], w_uq[j], w_uk[j], w_uv[j], w_mla_out[j])
        else:
            wi = w_mlstm_in[j]
            pad = LANES - 2 * heads_m
            w_ml = bf(jnp.pad(wi, ((0, 0), (0, pad))))
            bias = jnp.pad(jnp.concatenate([b_igate[j], b_fgate[j]]), (0, pad)).reshape(1, LANES)
            w_mo = bf(w_mlstm_out[j])
        for g in range(2):
            x = xs[g]
            nb = x.shape[0]
            x = _ffn(x, g_ffa[i], *w_ffa)
            if i % 2 == 0:
                tab = _rope_table(positions[g], mla["rope"])
                q, k, ckv, kr = _mla_proj(x, g_mix[i], mla, tab)
                if g == 0:
                    x = _attn_prompt(q, k, x, mla)
                else:
                    o = _attn_decode(q, k, cache_ckv, krope_t, j, page_table, mla["kv_rank"], mla["rope"])
                    x = _mla_out(o, x, mla)
                outs["ckv"][g].append(ckv)
                outs["kr"][g].append(kr)
            else:
                if g == 0:
                    c0 = jnp.zeros((nb, heads_m, dk, dv), F32)
                    n0 = jnp.zeros((nb, heads_m, dk), F32)
                    m0 = jnp.zeros((nb, heads_m), F32)
                else:
                    c0, n0, m0 = state_C[j], state_n[j], state_m[j]
                m0 = jnp.broadcast_to(m0[:, :, None], (nb, heads_m, LANES))
                qm, km, vm, om, gt = _mlstm_proj(x, g_mix[i], w_ml, qk, hv, dk)
                x, c_new, n_new, m_new = _mlstm(qm, km, vm, om, gt, x, c0, n0, m0, bias, g_mlstm_head[j], w_mo,
                                                nb=1 if g == 0 else 4)
                outs["C"][g].append(c_new)
                outs["n"][g].append(n_new)
                outs["m"][g].append(m_new[:, :, 0])
            x = _ffn(x, g_ffb[i], *w_ffb)
            x = _ple(x, ps[g][i], g_ple[i], w_gate, w_proj, g_final, final=(i == depth - 1))
            xs[g] = x

    st = lambda name, g: jnp.stack(outs[name][g])
    return (xs[0], xs[1], st("ckv", 0), st("kr", 0), st("ckv", 1), st("kr", 1),
            st("C", 0), st("n", 0), st("m", 0), st("C", 1), st("n", 1), st("m", 1))
```

```python
import functools
import math

import jax
import jax.numpy as jnp
import numpy as np
from jax import lax
from jax.experimental import pallas as pl
from jax.experimental.pallas import tpu as pltpu

NORM_EPS = 1e-6
ROPE_BASE = 10000.0
GATE_CAP = 15.0

LANES = 128
MXU_TILE = 256
VMEM_LIMIT = 56 * 1024 * 1024

F32 = jnp.float32
BF16 = jnp.bfloat16
HIGHEST = lax.Precision.HIGHEST


def _cparams(*sem):
    return pltpu.CompilerParams(dimension_semantics=sem, vmem_limit_bytes=VMEM_LIMIT)


def _dot(a, b):
    return jnp.dot(a, b, preferred_element_type=F32)


def _dot_nt(a, b, precision=None):
    return lax.dot_general(a, b, (((1,), (1,)), ((), ())), precision=precision,
                           preferred_element_type=F32)


def _dot_tn(a, b):
    return lax.dot_general(a, b, (((0,), (0,)), ((), ())), preferred_element_type=F32)


def _rmsnorm(x, g):
    return x * lax.rsqrt(jnp.mean(x * x, axis=-1, keepdims=True) + NORM_EPS) * g


def _full(shape):
    zeros = (0,) * len(shape)
    return pl.BlockSpec(shape, lambda *_: zeros)


def _tok_spec(bb, tt, width):
    return pl.BlockSpec((bb, tt, width), lambda b, t, *_: (b, t, 0))


def _token_tiles(B, T, rows):
    if T >= rows:
        return 1, rows
    return min(B, rows // T), T


def _ffn_body(x_ref, g_ref, win_ref, wo_ref, *rest, sub, ple, final):
    if ple:
        p_ref, gp_ref, wg_ref, wp_ref, gf_ref, o_ref = rest
    else:
        (o_ref,) = rest
    bb, tt, d = x_ref.shape
    f = wo_ref.shape[0]
    nb = sub // tt if bb > 1 else 1

    def rows_of(ref, r):
        if bb == 1:
            return ref[0, r * sub:(r + 1) * sub, :]
        return ref[r * nb:(r + 1) * nb].reshape(sub, ref.shape[-1])

    for r in range(bb * tt // sub):
        x = rows_of(x_ref, r)
        h = _rmsnorm(x, g_ref[...]).astype(BF16)
        a = _dot(h, win_ref[:, :f])
        b = _dot(h, win_ref[:, f:])
        act = (a * jax.nn.sigmoid(a) * b).astype(BF16)
        y = x + 0.5 * _dot(act, wo_ref[...])
        if ple:
            hn = _rmsnorm(y, gp_ref[...]).astype(BF16)
            gate = jax.nn.sigmoid(_dot(hn, wg_ref[...]))
            y = y + gate * _dot(rows_of(p_ref, r).astype(BF16), wp_ref[...])
            if final:
                y = _rmsnorm(y, gf_ref[...])
        if bb == 1:
            o_ref[0, r * sub:(r + 1) * sub, :] = y
        else:
            o_ref[r * nb:(r + 1) * nb] = y.reshape(nb, tt, d)


def _ffn(x, g, w_in, w_out, ple=None, rows=512, sub=256):
    B, T, D = x.shape
    F = w_out.shape[0]
    bb, tt = _token_tiles(B, T, rows)
    sub = min(sub, bb * tt)
    resident = lambda shape: pl.BlockSpec(shape, lambda b, t: (0,) * len(shape),
                                          pipeline_mode=pl.Buffered(1))
    args = [x, g.reshape(1, D), w_in, w_out]
    in_specs = [_tok_spec(bb, tt, D), _full((1, D)), resident((D, 2 * F)), resident((F, D))]
    final = False
    if ple is not None:
        p, g_ple, w_gate, w_proj, g_final, final = ple
        P = p.shape[-1]
        args += [p, g_ple.reshape(1, D), w_gate, w_proj, g_final.reshape(1, D)]
        in_specs += [_tok_spec(bb, tt, P), _full((1, D)), resident((D, D)), resident((P, D)), _full((1, D))]
    return pl.pallas_call(
        functools.partial(_ffn_body, sub=sub, ple=ple is not None, final=final),
        out_shape=jax.ShapeDtypeStruct(x.shape, F32),
        grid=(B // bb, T // tt),
        in_specs=in_specs,
        out_specs=_tok_spec(bb, tt, D),
        compiler_params=_cparams("parallel", "parallel"),
        name="ffn_ple" if ple is not None else "ffn",
    )(*args)


def _mla_proj_body(x_ref, gm_ref, win_ref, gq_ref, gkv_ref, wqn_ref, wqr_ref, wuk_ref, tab_ref,
                   q_ref, k_ref, ckv_ref, kr_ref, *, scale, q_rank, kv_rank, rope):
    bb, tt = x_ref.shape[0], x_ref.shape[1]
    rows = bb * tt
    heads = q_ref.shape[1]
    x = x_ref[...].reshape(rows, -1)
    h = _rmsnorm(x, gm_ref[...]).astype(BF16)
    z = _dot(h, win_ref[...])
    cq = _rmsnorm(z[:, :q_rank], gq_ref[...]).astype(BF16)
    ckv = _rmsnorm(z[:, q_rank:q_rank + kv_rank], gkv_ref[...])
    tab = jnp.broadcast_to(tab_ref[...][None], (bb, tt, LANES)).reshape(rows, LANES)
    yk = z[:, q_rank + kv_rank:] * tab
    kr = yk + pltpu.roll(yk, LANES - rope, 1)
    lane = lax.broadcasted_iota(jnp.int32, (rows, LANES), 1)
    kdup = jnp.where(lane < rope, kr, jnp.where(lane < 2 * rope, pltpu.roll(kr, rope, 1), 0.0))
    ckv_ref[...] = ckv.reshape(ckv_ref.shape)
    kr_ref[...] = kr[:, :rope].reshape(kr_ref.shape)
    k_ref[:, :, :kv_rank] = ckv.astype(BF16).reshape(bb, tt, kv_rank)
    k_ref[:, :, kv_rank:] = kdup.astype(BF16).reshape(bb, tt, LANES)

    qn = _dot(cq, wqn_ref[...]).astype(BF16)
    per = heads // (qn.shape[1] // MXU_TILE)
    for hd in range(heads):
        c0 = (hd // per) * MXU_TILE
        qa = _dot(qn[:, c0:c0 + MXU_TILE], wuk_ref[hd]) * scale
        qy = _dot(cq, wqr_ref[hd]) * tab * scale
        q_ref[:, hd, :, :kv_rank] = qa.astype(BF16).reshape(bb, tt, kv_rank)
        q_ref[:, hd, :, kv_rank:] = qy.astype(BF16).reshape(bb, tt, LANES)


def _mla_proj(x, g_mix, w, tab, rows=256):
    B, T, D = x.shape
    heads, q_rank, kv_rank, rope = w["heads"], w["q_rank"], w["kv_rank"], w["rope"]
    bb, tt = _token_tiles(B, T, rows)
    kw = kv_rank + LANES
    body = functools.partial(_mla_proj_body, scale=w["scale"], q_rank=q_rank, kv_rank=kv_rank, rope=rope)
    return pl.pallas_call(
        body,
        out_shape=(
            jax.ShapeDtypeStruct((B, heads, T, kw), BF16),
            jax.ShapeDtypeStruct((B, T, kw), BF16),
            jax.ShapeDtypeStruct((B, T, kv_rank), F32),
            jax.ShapeDtypeStruct((B, T, rope), F32),
        ),
        grid=(B // bb, T // tt),
        in_specs=[
            _tok_spec(bb, tt, D), _full((1, D)), _full(w["w_in"].shape), _full((1, q_rank)),
            _full((1, kv_rank)), _full(w["w_qn"].shape), _full(w["w_qr"].shape), _full(w["w_uk"].shape),
            pl.BlockSpec((tt, LANES), lambda b, t: (t, 0)),
        ],
        out_specs=(
            pl.BlockSpec((bb, heads, tt, kw), lambda b, t: (b, 0, t, 0)),
            _tok_spec(bb, tt, kw), _tok_spec(bb, tt, kv_rank), _tok_spec(bb, tt, rope),
        ),
        compiler_params=_cparams("parallel", "parallel"),
        name="mla_proj",
    )(x, g_mix.reshape(1, D), w["w_in"], w["g_q"].reshape(1, q_rank), w["g_kv"].reshape(1, kv_rank),
      w["w_qn"], w["w_qr"], w["w_uk"], tab)


def _prep_mla(w_mla_in, g_q, g_kv, w_uq, w_uk, w_uv, w_out):
    q_rank, heads, qd = w_uq.shape
    kv_rank, _, nope = w_uk.shape
    vdim = w_uv.shape[2]
    rope = qd - nope
    half = rope // 2
    assert kv_rank == MXU_TILE and MXU_TILE % nope == 0 and MXU_TILE % vdim == 0 and 2 * rope <= LANES
    d = w_mla_in.shape[0]
    wr = w_mla_in[:, q_rank + kv_rank:]
    w_in = jnp.concatenate(
        [w_mla_in, wr[:, half:], wr[:, :half], jnp.zeros((d, LANES - 2 * rope), F32)], axis=1).astype(BF16)
    w_qn = w_uq[:, :, :nope].reshape(q_rank, heads * nope).astype(BF16)
    x1, x2 = w_uq[:, :, nope:nope + half], w_uq[:, :, nope + half:]
    w_qr = jnp.concatenate([x1, x2, x2, x1, jnp.zeros((q_rank, heads, LANES - 2 * rope), F32)], axis=2)
    w_qr = jnp.transpose(w_qr, (1, 0, 2)).astype(BF16)
    per = MXU_TILE // nope
    slot = jax.nn.one_hot(jnp.arange(heads) % per, per, dtype=F32)
    w_ukp = jnp.einsum("rhd,hs->hsdr", w_uk, slot).reshape(heads, MXU_TILE, kv_rank).astype(BF16)
    perv = MXU_TILE // vdim
    slotv = jax.nn.one_hot(jnp.arange(heads) % perv, perv, dtype=F32)
    w_uvp = jnp.einsum("rhv,hs->hrsv", w_uv, slotv).reshape(heads, kv_rank, MXU_TILE).astype(BF16)
    return dict(heads=heads, q_rank=q_rank, kv_rank=kv_rank, rope=rope, nope=nope, vdim=vdim,
                scale=float((nope + rope) ** -0.5 * math.log2(math.e)), w_in=w_in, g_q=g_q, g_kv=g_kv, w_qn=w_qn,
                w_qr=w_qr, w_uk=w_ukp, w_uv=w_uvp, w_out=w_out.astype(BF16))


def _rope_table(pos, rope):
    half = rope // 2
    inv = ROPE_BASE ** (-jnp.arange(half, dtype=F32) / half)
    ang = pos.astype(F32)[:, None] * inv[None, :]
    c, s = jnp.cos(ang), jnp.sin(ang)
    return jnp.concatenate([c, c, -s, s, jnp.zeros((pos.shape[0], LANES - 2 * rope), F32)], axis=1)


def _lane_tile(stat, width):
    if width % LANES:
        return stat[:, :1]
    return stat if width == LANES else jnp.tile(stat, (1, width // LANES))


def _softmax_step(s, v, m_scr, l_scr, acc_scr):
    m_prev = m_scr[...]
    m_new = jnp.maximum(m_prev, jnp.max(s, axis=1, keepdims=True))
    alpha = jnp.exp2(m_prev - m_new)
    p = jnp.exp2(s - _lane_tile(m_new, s.shape[1]))
    l_scr[...] = alpha * l_scr[...] + jnp.sum(p, axis=1, keepdims=True)
    acc_scr[...] = acc_scr[...] * _lane_tile(alpha, v.shape[1]) + _dot(p.astype(BF16), v)
    m_scr[...] = m_new


def _softmax_result(l_scr, acc_scr):
    return acc_scr[...] / _lane_tile(l_scr[...], acc_scr.shape[-1])


def _softmax_init(m_scr, l_scr, acc_scr):
    m_scr[...] = jnp.full_like(m_scr, -jnp.inf)
    l_scr[...] = jnp.zeros_like(l_scr)
    acc_scr[...] = jnp.zeros_like(acc_scr)


def _attn_prompt_body(qi_ref, kj_ref, q_ref, k_ref, x_ref, wuv_ref, wo_ref, y_ref, m_scr, l_scr, acc_scr,
                      *, tq, tk, kv_rank, ahead):
    s_id = pl.program_id(1)
    qi, kj = qi_ref[s_id], kj_ref[s_id]
    heads = q_ref.shape[1]
    groups, rows = m_scr.shape[0], m_scr.shape[1]
    hb = heads // groups

    @pl.when(kj == 0)
    def _():
        _softmax_init(m_scr, l_scr, acc_scr)

    k = k_ref[0]
    v = k[:, :kv_rank]
    qpos = qi * tq + lax.broadcasted_iota(jnp.int32, (tq, tk), 0)
    kpos = kj * tk + lax.broadcasted_iota(jnp.int32, (tq, tk), 1)
    bias = jnp.where(kpos <= qpos, 0.0, -jnp.inf)

    def scores(g):
        q = q_ref[0, g * hb:(g + 1) * hb].reshape(rows, -1)
        return (_dot_nt(q, k).reshape(hb, tq, tk) + bias[None]).reshape(rows, tk)

    pending = [scores(g) for g in range(min(ahead, groups))]
    for g in range(groups):
        if g + ahead < groups:
            pending.append(scores(g + ahead))
        _softmax_step(pending.pop(0), v, m_scr.at[g], l_scr.at[g], acc_scr.at[g])

    @pl.when(kj == ((qi + 1) * tq - 1) // tk)
    def _():
        def latent_out(hd):
            g, r = divmod(hd, hb)
            o = _softmax_result(l_scr.at[g], acc_scr.at[g])[r * tq:(r + 1) * tq]
            return o.astype(BF16)

        y_ref[0] = _mla_project_out(latent_out, heads, x_ref[0], wuv_ref, wo_ref)


def _mla_project_out(latent_out, heads, x, wuv_ref, wo_ref):
    groups = wo_ref.shape[0] // MXU_TILE
    per = heads // groups
    vs = []
    for g in range(groups):
        acc = None
        for hd in range(g * per, (g + 1) * per):
            part = _dot(latent_out(hd), wuv_ref[hd])
            acc = part if acc is None else acc + part
        vs.append(acc.astype(BF16))
    v = jnp.concatenate(vs, axis=1)
    return x + _dot(v, wo_ref[...])


def _attn_prompt(q, k, x, w, tq=512, tk=512, hb=1, ahead=1):
    kv_rank = w["kv_rank"]
    D = x.shape[-1]
    resident = pl.Buffered(1)
    B, heads, T, kw = q.shape
    tq, tk = min(tq, T), min(tk, T)
    groups, rows = heads // hb, hb * tq
    pairs = [(i, j) for i in range(T // tq) for j in range(((i + 1) * tq - 1) // tk + 1)]
    qi = jnp.asarray(np.array([p[0] for p in pairs], np.int32))
    kj = jnp.asarray(np.array([p[1] for p in pairs], np.int32))
    grid_spec = pltpu.PrefetchScalarGridSpec(
        num_scalar_prefetch=2,
        grid=(B, len(pairs)),
        in_specs=[
            pl.BlockSpec((1, heads, tq, kw), lambda b, s, qi, kj: (b, 0, qi[s], 0)),
            pl.BlockSpec((1, tk, kw), lambda b, s, qi, kj: (b, kj[s], 0)),
            pl.BlockSpec((1, tq, D), lambda b, s, qi, kj: (b, qi[s], 0)),
            pl.BlockSpec(w["w_uv"].shape, lambda b, s, qi, kj: (0, 0, 0), pipeline_mode=resident),
            pl.BlockSpec(w["w_out"].shape, lambda b, s, qi, kj: (0, 0), pipeline_mode=resident),
        ],
        out_specs=pl.BlockSpec((1, tq, D), lambda b, s, qi, kj: (b, qi[s], 0)),
        scratch_shapes=[pltpu.VMEM((groups, rows, LANES), F32), pltpu.VMEM((groups, rows, LANES), F32),
                        pltpu.VMEM((groups, rows, kv_rank), F32)],
    )
    return pl.pallas_call(
        functools.partial(_attn_prompt_body, tq=tq, tk=tk, kv_rank=kv_rank, ahead=ahead),
        out_shape=jax.ShapeDtypeStruct(x.shape, F32),
        grid_spec=grid_spec,
        compiler_params=_cparams("parallel", "arbitrary"),
        name="attn_prompt",
    )(qi, kj, q, k, x, w["w_uv"], w["w_out"])


def _attn_decode_body(pt_ref, q_ref, kn_ref, ck_hbm, kr_hbm, o_ref, ck_buf, kr_buf, sem, m_scr, l_scr, acc_scr,
                      *, layer, kv_rank, rope):
    step, nsteps = pl.program_id(0), pl.num_programs(0)
    nb, heads, tn = q_ref.shape[0], q_ref.shape[1], q_ref.shape[2]
    pages, page = ck_buf.shape[2], ck_buf.shape[3]
    n_chunks = pt_ref.shape[1] // pages
    rows = heads * tn

    def page_copies(slot, page_id):
        copies = []
        for e in range(nb):
            for j in range(pages):
                pid = page_id(e, j)
                copies.append(pltpu.make_async_copy(ck_hbm.at[layer, pid], ck_buf.at[slot, e, j], sem.at[slot]))
                copies.append(pltpu.make_async_copy(kr_hbm.at[layer, pid], kr_buf.at[slot, e, j], sem.at[slot]))
        return copies

    def start_chunk(slot, at_step, chunk):
        for c in page_copies(slot, lambda e, j: pt_ref[at_step * nb + e, chunk * pages + j]):
            c.start()

    def wait_chunk(slot):
        for c in page_copies(slot, lambda e, j: 0):
            c.wait()

    @pl.when(step == 0)
    def _():
        start_chunk(0, 0, 0)

    _softmax_init(m_scr, l_scr, acc_scr)

    def queries(e):
        q = q_ref[e].reshape(rows, -1)
        qy = q[:, kv_rank:].astype(F32)
        qr = (qy + pltpu.roll(qy, LANES - rope, 1))[:, :rope].astype(BF16)
        return q, q[:, :kv_rank], qr

    def attend(slot):
        scores = []
        for e in range(nb):
            _, qa, qr = queries(e)
            kc = ck_buf[slot, e].reshape(pages * page, kv_rank).astype(BF16)
            krt = jnp.concatenate([kr_buf[slot, e, j].astype(BF16) for j in range(pages)], axis=1)
            scores.append((_dot_nt(qa, kc) + _dot(qr, krt), kc))
        for e, (s, kc) in enumerate(scores):
            _softmax_step(s, kc, m_scr.at[e], l_scr.at[e], acc_scr.at[e])

    def chunk_pair(c2, carry):
        start_chunk(1, step, 2 * c2 + 1)
        wait_chunk(0)
        attend(0)
        last = c2 == n_chunks // 2 - 1
        nxt_step = jnp.where(last, step + 1, step)
        nxt_chunk = jnp.where(last, 0, 2 * c2 + 2)

        @pl.when(nxt_step < nsteps)
        def _():
            start_chunk(0, nxt_step, nxt_chunk)

        wait_chunk(1)
        attend(1)
        return carry

    lax.fori_loop(0, n_chunks // 2, chunk_pair, 0)

    for e in range(nb):
        q, _, _ = queries(e)
        kn = kn_ref[e]
        s = _dot_nt(q, kn)
        tq_ = lax.broadcasted_iota(jnp.int32, (heads, tn, tn), 1).reshape(rows, tn)
        tk_ = lax.broadcasted_iota(jnp.int32, (rows, tn), 1)
        s = jnp.where(tk_ <= tq_, s, -jnp.inf)
        _softmax_step(s, kn[:, :kv_rank], m_scr.at[e], l_scr.at[e], acc_scr.at[e])
        o = _softmax_result(l_scr.at[e], acc_scr.at[e])
        o_ref[e] = o.astype(BF16).reshape(heads, tn, kv_rank)


def _attn_decode(q, k_new, cache_ckv, cache_krope_t, layer, page_table, kv_rank, rope, pages=16, nb=4):
    B, heads, tn, kw = q.shape
    n_pages = page_table.shape[1]
    page = cache_ckv.shape[2]
    nb = math.gcd(nb, B)
    pages = math.gcd(pages, n_pages // 2)
    assert (n_pages // pages) % 2 == 0, "the two-slot ring walks chunks in pairs"
    rows = heads * tn
    grid_spec = pltpu.PrefetchScalarGridSpec(
        num_scalar_prefetch=1,
        grid=(B // nb,),
        in_specs=[pl.BlockSpec((nb, heads, tn, kw), lambda b, pt: (b, 0, 0, 0)),
                  pl.BlockSpec((nb, tn, kw), lambda b, pt: (b, 0, 0)),
                  pl.BlockSpec(memory_space=pl.ANY),
                  pl.BlockSpec(memory_space=pl.ANY)],
        out_specs=pl.BlockSpec((nb, heads, tn, kv_rank), lambda b, pt: (b, 0, 0, 0)),
        scratch_shapes=[pltpu.VMEM((2, nb, pages, page, kv_rank), cache_ckv.dtype),
                        pltpu.VMEM((2, nb, pages, rope, page), cache_krope_t.dtype),
                        pltpu.SemaphoreType.DMA((2,)),
                        pltpu.VMEM((nb, rows, LANES), F32), pltpu.VMEM((nb, rows, LANES), F32),
                        pltpu.VMEM((nb, rows, kv_rank), F32)],
    )
    return pl.pallas_call(
        functools.partial(_attn_decode_body, layer=layer, kv_rank=kv_rank, rope=rope),
        out_shape=jax.ShapeDtypeStruct((B, heads, tn, kv_rank), BF16),
        grid_spec=grid_spec,
        compiler_params=_cparams("arbitrary"),
        name="attn_decode",
    )(page_table, q, k_new, cache_ckv, cache_krope_t)


def _mla_out_body(o_ref, x_ref, wuv_ref, wo_ref, y_ref):
    bb, heads, tt, kv_rank = o_ref.shape
    rows = bb * tt
    x = x_ref[...].reshape(rows, -1)
    y = _mla_project_out(lambda hd: o_ref[:, hd].reshape(rows, kv_rank), heads, x, wuv_ref, wo_ref)
    y_ref[...] = y.reshape(y_ref.shape)


def _mla_out(o, x, w, rows=256):
    B, T, D = x.shape
    heads, kv_rank = o.shape[1], o.shape[3]
    bb, tt = _token_tiles(B, T, rows)
    return pl.pallas_call(
        _mla_out_body,
        out_shape=jax.ShapeDtypeStruct(x.shape, F32),
        grid=(B // bb, T // tt),
        in_specs=[pl.BlockSpec((bb, heads, tt, kv_rank), lambda b, t: (b, 0, t, 0)),
                  _tok_spec(bb, tt, D), _full(w["w_uv"].shape), _full(w["w_out"].shape)],
        out_specs=_tok_spec(bb, tt, D),
        compiler_params=_cparams("parallel", "parallel"),
        name="mla_out",
    )(o, x, w["w_uv"], w["w_out"])


def _mlstm_proj_body(x_ref, g_ref, w_ref, q_ref, k_ref, v_ref, o_ref, gt_ref, *, qk, hv, qscale):
    rows = x_ref.shape[0] * x_ref.shape[1]
    x = x_ref[...].reshape(rows, -1)
    h = _rmsnorm(x, g_ref[...]).astype(BF16)
    z = _dot(h, w_ref[...])
    q_ref[...] = (z[:, :qk] * qscale).astype(BF16).reshape(q_ref.shape)
    k_ref[...] = z[:, qk:2 * qk].astype(BF16).reshape(k_ref.shape)
    v_ref[...] = z[:, 2 * qk:2 * qk + hv].astype(BF16).reshape(v_ref.shape)
    o_ref[...] = z[:, 2 * qk + hv:2 * qk + 2 * hv].reshape(o_ref.shape)
    gt_ref[...] = z[:, 2 * qk + 2 * hv:].reshape(gt_ref.shape)


def _mlstm_proj(x, g, w, qk, hv, dk, rows=512):
    B, T, D = x.shape
    bb, tt = _token_tiles(B, T, rows)
    widths = (qk, qk, hv, hv, LANES)
    dtypes = (BF16, BF16, BF16, F32, F32)
    return pl.pallas_call(
        functools.partial(_mlstm_proj_body, qk=qk, hv=hv, qscale=float(dk ** -0.5)),
        out_shape=tuple(jax.ShapeDtypeStruct((B, T, wd), dt) for wd, dt in zip(widths, dtypes)),
        grid=(B // bb, T // tt),
        in_specs=[_tok_spec(bb, tt, D), _full((1, D)), _full(w.shape)],
        out_specs=tuple(_tok_spec(bb, tt, wd) for wd in widths),
        compiler_params=_cparams("parallel", "parallel"),
        name="mlstm_proj",
    )(x, g.reshape(1, D), w)


def _mlstm_body(q_ref, k_ref, v_ref, o_ref, gt_ref, x_ref, c0_ref, n0_ref, m0_ref, bias_ref, gh_ref, wo_ref,
                y_ref, c_ref, n_ref, m_ref, c_scr, n_scr, m_scr, *, heads, dk, dv):
    ci = pl.program_id(1)
    nb, L = q_ref.shape[0], q_ref.shape[1]

    @pl.when(ci == 0)
    def _():
        c_scr[...] = c0_ref[...]
        n_scr[...] = n0_ref[...]
        m_scr[...] = m0_ref[...]

    lane = lax.broadcasted_iota(jnp.int32, (L, LANES), 1)
    row = lax.broadcasted_iota(jnp.int32, (L, L), 0)
    col = lax.broadcasted_iota(jnp.int32, (L, L), 1)
    causal = col <= row
    tri = causal.astype(F32)

    gates, pre = [], {}
    for e in range(nb):
        a = GATE_CAP * jnp.tanh((gt_ref[e] + bias_ref[...]) / GATE_CAP)
        lf = -(jnp.maximum(-a, 0.0) + jnp.log1p(jnp.exp(-jnp.abs(a))))
        xg = jnp.where(lane < heads, a, jnp.where(lane < 2 * heads, lf, 0.0))
        bc = jnp.dot(tri, xg, precision=HIGHEST, preferred_element_type=F32)
        gates.append((xg, bc, xg.T, bc.T))
    for e in range(nb):
        for hd in range(heads):
            qh = q_ref[e, :, hd * dk:(hd + 1) * dk]
            kh = k_ref[e, :, hd * dk:(hd + 1) * dk]
            pre[e, hd] = (_dot_nt(qh, kh), _dot(qh, c_scr[e, hd].astype(BF16)))

    for e in range(nb):
        xg, bc, xg_t, bc_t = gates[e]
        blast_row = bc[L - 1:L, :]
        outs, post = [], []
        for hd in range(heads):
            qk_, qc = pre[e, hd]
            b_col = jnp.sum(jnp.where(lane == heads + hd, bc, 0.0), axis=1, keepdims=True)
            ib_row = xg_t[hd:hd + 1, :] - bc_t[heads + hd:heads + hd + 1, :]
            log_d = jnp.where(causal, b_col + ib_row, -jnp.inf)
            i_col = jnp.sum(jnp.where(lane == hd, xg, 0.0), axis=1, keepdims=True)
            b_last = jnp.sum(jnp.where(lane[:1] == heads + hd, blast_row, 0.0), axis=1, keepdims=True)
            m_prev = m_scr[e, hd:hd + 1, :1]
            m_t = jnp.maximum(b_col + m_prev, jnp.max(log_d, axis=1, keepdims=True))
            decay = jnp.exp(b_col + m_prev - m_t)
            qh = q_ref[e, :, hd * dk:(hd + 1) * dk]
            kh = k_ref[e, :, hd * dk:(hd + 1) * dk]
            vh = v_ref[e, :, hd * dv:(hd + 1) * dv]
            c_prev = c_scr[e, hd]
            n_prev = n_scr[e, hd:hd + 1, :]
            sm = qk_ * jnp.exp(log_d - m_t)
            num = _dot(sm.astype(BF16), vh) + decay * qc
            den = (jnp.sum(sm, axis=1, keepdims=True)
                   + decay * jnp.sum(qh.astype(F32) * n_prev, axis=1, keepdims=True))
            hh = num / jnp.maximum(jnp.abs(den), jnp.exp(-m_t))
            hh = (hh * lax.rsqrt(jnp.mean(hh * hh, axis=1, keepdims=True) + NORM_EPS)
                  * gh_ref[:, hd * dv:(hd + 1) * dv])
            outs.append((jax.nn.sigmoid(o_ref[e, :, hd * dv:(hd + 1) * dv]) * hh).astype(BF16))
            post.append((hd, m_t, m_prev, i_col, b_col, b_last, kh, vh, c_prev, n_prev))
        for hd, m_t, m_prev, i_col, b_col, b_last, kh, vh, c_prev, n_prev in post:
            m_new = m_t[L - 1:L, :]
            wgt = jnp.exp(i_col + b_last - b_col - m_new)
            cdec = jnp.exp(b_last + m_prev - m_new)
            kw_ = kh.astype(F32) * wgt
            c_scr[e, hd] = cdec * c_prev + _dot_tn(kw_.astype(BF16), vh)
            n_scr[e, hd:hd + 1, :] = cdec * n_prev + jnp.sum(kw_, axis=0, keepdims=True)
            m_scr[e, hd:hd + 1, :] = jnp.broadcast_to(m_new, (1, LANES))
        out = jnp.concatenate(outs, axis=1)
        y_ref[e] = x_ref[e] + _dot(out, wo_ref[...])

    @pl.when(ci == pl.num_programs(1) - 1)
    def _():
        c_ref[...] = c_scr[...]
        n_ref[...] = n_scr[...]
        m_ref[...] = m_scr[...]


def _mlstm(q, k, v, o, gt, x, c0, n0, m0, bias, g_head, w_out, chunk=256, nb=1):
    B, T, D = x.shape
    heads, dk, dv = c0.shape[1], c0.shape[2], c0.shape[3]
    L = min(chunk, T)
    nb = math.gcd(nb, B)
    tok = lambda wd: pl.BlockSpec((nb, L, wd), lambda b, c: (b, c, 0))
    st3 = pl.BlockSpec((nb, heads, LANES), lambda b, c: (b, 0, 0))
    st4 = pl.BlockSpec((nb, heads, dk, dv), lambda b, c: (b, 0, 0, 0))
    return pl.pallas_call(
        functools.partial(_mlstm_body, heads=heads, dk=dk, dv=dv),
        out_shape=(jax.ShapeDtypeStruct(x.shape, F32),
                   jax.ShapeDtypeStruct(c0.shape, F32),
                   jax.ShapeDtypeStruct((B, heads, dk), F32),
                   jax.ShapeDtypeStruct((B, heads, LANES), F32)),
        grid=(B // nb, T // L),
        in_specs=[tok(heads * dk), tok(heads * dk), tok(heads * dv), tok(heads * dv), tok(LANES), tok(D),
                  st4, st3, st3, _full((1, LANES)), _full((1, heads * dv)), _full(w_out.shape)],
        out_specs=(tok(D), st4, st3, st3),
        scratch_shapes=[pltpu.VMEM((nb, heads, dk, dv), F32), pltpu.VMEM((nb, heads, dk), F32),
                        pltpu.VMEM((nb, heads, LANES), F32)],
        compiler_params=_cparams("parallel", "arbitrary"),
        name="mlstm",
    )(q, k, v, o, gt, x, c0, n0, m0, bias, g_head.reshape(1, heads * dv), w_out)


def kernel(x_prompt, x_sample, cache_ckv, cache_krope, state_C, state_n, state_m, page_table, p_prompt, p_sample, g_ffa, w_ffa_in, w_ffa_out, g_mix, g_ffb, w_ffb_in, w_ffb_out, g_ple, w_ple_gate, w_ple_proj, w_mla_in, g_q_lat, g_kv_lat, w_uq, w_uk, w_uv, w_mla_out, w_mlstm_in, b_igate, b_fgate, g_mlstm_head, w_mlstm_out, g_final):
    depth = g_ffa.shape[0]
    seq, dec_seq = x_prompt.shape[1], x_sample.shape[1]
    past = page_table.shape[1] * cache_ckv.shape[2]
    positions = (jnp.arange(seq), past + jnp.arange(dec_seq))
    xs = [x_prompt, x_sample]
    ps = (p_prompt, p_sample)
    heads_m, dk, dv = state_C.shape[2], state_C.shape[3], state_C.shape[4]
    assert dk == LANES and 2 * heads_m <= LANES
    qk, hv = heads_m * dk, heads_m * dv
    outs = {name: ([], []) for name in ("ckv", "kr", "C", "n", "m")}
    krope_t = jnp.swapaxes(cache_krope, 2, 3)

    for i in range(depth):
        j = i // 2
        bf = lambda a: a.astype(BF16)
        w_ffa = (bf(w_ffa_in[i]), bf(w_ffa_out[i]))
        w_ffb = (bf(w_ffb_in[i]), bf(w_ffb_out[i]))
        w_gate, w_proj = bf(w_ple_gate[i]), bf(w_ple_proj[i])
        if i % 2 == 0:
            mla = _prep_mla(w_mla_in[j], g_q_lat[j], g_kv_lat[j], w_uq[j], w_uk[j], w_uv[j], w_mla_out[j])
        else:
            wi = w_mlstm_in[j]
            pad = LANES - 2 * heads_m
            w_ml = bf(jnp.pad(wi, ((0, 0), (0, pad))))
            bias = jnp.pad(jnp.concatenate([b_igate[j], b_fgate[j]]), (0, pad)).reshape(1, LANES)
            w_mo = bf(w_mlstm_out[j])
        for g in range(2):
            x = xs[g]
            nb = x.shape[0]
            x = _ffn(x, g_ffa[i], *w_ffa)
            if i % 2 == 0:
                tab = _rope_table(positions[g], mla["rope"])
                q, k, ckv, kr = _mla_proj(x, g_mix[i], mla, tab)
                if g == 0:
                    x = _attn_prompt(q, k, x, mla)
                else:
                    o = _attn_decode(q, k, cache_ckv, krope_t, j, page_table, mla["kv_rank"], mla["rope"])
                    x = _mla_out(o, x, mla)
                outs["ckv"][g].append(ckv)
                outs["kr"][g].append(kr)
            else:
                if g == 0:
                    c0 = jnp.zeros((nb, heads_m, dk, dv), F32)
                    n0 = jnp.zeros((nb, heads_m, dk), F32)
                    m0 = jnp.zeros((nb, heads_m), F32)
                else:
                    c0, n0, m0 = state_C[j], state_n[j], state_m[j]
                m0 = jnp.broadcast_to(m0[:, :, None], (nb, heads_m, LANES))
                qm, km, vm, om, gt = _mlstm_proj(x, g_mix[i], w_ml, qk, hv, dk)
                x, c_new, n_new, m_new = _mlstm(qm, km, vm, om, gt, x, c0, n0, m0, bias, g_mlstm_head[j], w_mo,
                                                nb=1 if g == 0 else 4)
                outs["C"][g].append(c_new)
                outs["n"][g].append(n_new)
                outs["m"][g].append(m_new[:, :, 0])
            x = _ffn(x, g_ffb[i], *w_ffb, ple=(ps[g][i], g_ple[i], w_gate, w_proj, g_final, i == depth - 1))
            xs[g] = x

    st = lambda name, g: jnp.stack(outs[name][g])
    return (xs[0], xs[1], st("ckv", 0), st("kr", 0), st("ckv", 1), st("kr", 1),
            st("C", 0), st("n", 0), st("m", 0), st("C", 1), st("n", 1), st("m", 1))
```

```python
import functools
import math

import jax
import jax.numpy as jnp
import numpy as np
from jax import lax
from jax.experimental import pallas as pl
from jax.experimental.pallas import tpu as pltpu

NORM_EPS = 1e-6
ROPE_BASE = 10000.0
GATE_CAP = 15.0

LANES = 128
MXU_TILE = 256
VMEM_LIMIT = 56 * 1024 * 1024

F32 = jnp.float32
BF16 = jnp.bfloat16
HIGHEST = lax.Precision.HIGHEST


def _cparams(*sem):
    return pltpu.CompilerParams(dimension_semantics=sem, vmem_limit_bytes=VMEM_LIMIT)


def _dot(a, b):
    return jnp.dot(a, b, preferred_element_type=F32)


def _dot_nt(a, b, precision=None):
    return lax.dot_general(a, b, (((1,), (1,)), ((), ())), precision=precision,
                           preferred_element_type=F32)


def _dot_tn(a, b):
    return lax.dot_general(a, b, (((0,), (0,)), ((), ())), preferred_element_type=F32)


def _rmsnorm(x, g):
    return x * lax.rsqrt(jnp.mean(x * x, axis=-1, keepdims=True) + NORM_EPS) * g


def _full(shape):
    zeros = (0,) * len(shape)
    return pl.BlockSpec(shape, lambda *_: zeros)


def _tok_spec(bb, tt, width):
    return pl.BlockSpec((bb, tt, width), lambda b, t, *_: (b, t, 0))


def _token_tiles(B, T, rows):
    if T >= rows:
        return 1, rows
    return min(B, rows // T), T


def _ffn_body(x_ref, g_ref, win_ref, wo_ref, *rest, sub, ple, final):
    if ple:
        p_ref, gp_ref, wg_ref, wp_ref, gf_ref, o_ref = rest
    else:
        (o_ref,) = rest
    bb, tt, d = x_ref.shape
    f = wo_ref.shape[0]
    nb = sub // tt if bb > 1 else 1

    def rows_of(ref, r):
        if bb == 1:
            return ref[0, r * sub:(r + 1) * sub, :]
        return ref[r * nb:(r + 1) * nb].reshape(sub, ref.shape[-1])

    for r in range(bb * tt // sub):
        x = rows_of(x_ref, r)
        h = _rmsnorm(x, g_ref[...]).astype(BF16)
        a = _dot(h, win_ref[:, :f])
        b = _dot(h, win_ref[:, f:])
        act = (a * jax.nn.sigmoid(a) * b).astype(BF16)
        y = x + 0.5 * _dot(act, wo_ref[...])
        if ple:
            hn = _rmsnorm(y, gp_ref[...]).astype(BF16)
            gate = jax.nn.sigmoid(_dot(hn, wg_ref[...]))
            y = y + gate * _dot(rows_of(p_ref, r).astype(BF16), wp_ref[...])
            if final:
                y = _rmsnorm(y, gf_ref[...])
        if bb == 1:
            o_ref[0, r * sub:(r + 1) * sub, :] = y
        else:
            o_ref[r * nb:(r + 1) * nb] = y.reshape(nb, tt, d)


def _ffn(x, g, w_in, w_out, ple=None, rows=512, sub=256):
    B, T, D = x.shape
    F = w_out.shape[0]
    bb, tt = _token_tiles(B, T, rows)
    sub = min(sub, bb * tt)
    resident = lambda shape: pl.BlockSpec(shape, lambda b, t: (0,) * len(shape),
                                          pipeline_mode=pl.Buffered(1))
    args = [x, g.reshape(1, D), w_in, w_out]
    in_specs = [_tok_spec(bb, tt, D), _full((1, D)), resident((D, 2 * F)), resident((F, D))]
    final = False
    if ple is not None:
        p, g_ple, w_gate, w_proj, g_final, final = ple
        P = p.shape[-1]
        args += [p, g_ple.reshape(1, D), w_gate, w_proj, g_final.reshape(1, D)]
        in_specs += [_tok_spec(bb, tt, P), _full((1, D)), resident((D, D)), resident((P, D)), _full((1, D))]
    return pl.pallas_call(
        functools.partial(_ffn_body, sub=sub, ple=ple is not None, final=final),
        out_shape=jax.ShapeDtypeStruct(x.shape, F32),
        grid=(B // bb, T // tt),
        in_specs=in_specs,
        out_specs=_tok_spec(bb, tt, D),
        compiler_params=_cparams("parallel", "parallel"),
        name="ffn_ple" if ple is not None else "ffn",
    )(*args)


def _mla_proj_body(x_ref, gm_ref, win_ref, gq_ref, gkv_ref, wqn_ref, wqr_ref, wuk_ref, tab_ref,
                   q_ref, k_ref, ckv_ref, kr_ref, *, scale, q_rank, kv_rank, rope):
    bb, tt = x_ref.shape[0], x_ref.shape[1]
    rows = bb * tt
    heads = q_ref.shape[1]
    x = x_ref[...].reshape(rows, -1)
    h = _rmsnorm(x, gm_ref[...]).astype(BF16)
    z = _dot(h, win_ref[...])
    cq = _rmsnorm(z[:, :q_rank], gq_ref[...]).astype(BF16)
    ckv = _rmsnorm(z[:, q_rank:q_rank + kv_rank], gkv_ref[...])
    tab = jnp.broadcast_to(tab_ref[...][None], (bb, tt, LANES)).reshape(rows, LANES)
    yk = z[:, q_rank + kv_rank:] * tab
    kr = yk + pltpu.roll(yk, LANES - rope, 1)
    lane = lax.broadcasted_iota(jnp.int32, (rows, LANES), 1)
    kdup = jnp.where(lane < rope, kr, jnp.where(lane < 2 * rope, pltpu.roll(kr, rope, 1), 0.0))
    ckv_ref[...] = ckv.reshape(ckv_ref.shape)
    kr_ref[...] = kr[:, :rope].reshape(kr_ref.shape)
    k_ref[:, :, :kv_rank] = ckv.astype(BF16).reshape(bb, tt, kv_rank)
    k_ref[:, :, kv_rank:] = kdup.astype(BF16).reshape(bb, tt, LANES)

    qn = _dot(cq, wqn_ref[...]).astype(BF16)
    per = heads // (qn.shape[1] // MXU_TILE)
    for hd in range(heads):
        c0 = (hd // per) * MXU_TILE
        qa = _dot(qn[:, c0:c0 + MXU_TILE], wuk_ref[hd]) * scale
        qy = _dot(cq, wqr_ref[hd]) * tab * scale
        q_ref[:, hd, :, :kv_rank] = qa.astype(BF16).reshape(bb, tt, kv_rank)
        q_ref[:, hd, :, kv_rank:] = qy.astype(BF16).reshape(bb, tt, LANES)


def _mla_proj(x, g_mix, w, tab, rows=512):
    B, T, D = x.shape
    heads, q_rank, kv_rank, rope = w["heads"], w["q_rank"], w["kv_rank"], w["rope"]
    bb, tt = _token_tiles(B, T, rows)
    kw = kv_rank + LANES
    body = functools.partial(_mla_proj_body, scale=w["scale"], q_rank=q_rank, kv_rank=kv_rank, rope=rope)
    return pl.pallas_call(
        body,
        out_shape=(
            jax.ShapeDtypeStruct((B, heads, T, kw), BF16),
            jax.ShapeDtypeStruct((B, T, kw), BF16),
            jax.ShapeDtypeStruct((B, T, kv_rank), F32),
            jax.ShapeDtypeStruct((B, T, rope), F32),
        ),
        grid=(B // bb, T // tt),
        in_specs=[
            _tok_spec(bb, tt, D), _full((1, D)), _full(w["w_in"].shape), _full((1, q_rank)),
            _full((1, kv_rank)), _full(w["w_qn"].shape), _full(w["w_qr"].shape), _full(w["w_uk"].shape),
            pl.BlockSpec((tt, LANES), lambda b, t: (t, 0)),
        ],
        out_specs=(
            pl.BlockSpec((bb, heads, tt, kw), lambda b, t: (b, 0, t, 0)),
            _tok_spec(bb, tt, kw), _tok_spec(bb, tt, kv_rank), _tok_spec(bb, tt, rope),
        ),
        compiler_params=_cparams("parallel", "parallel"),
        name="mla_proj",
    )(x, g_mix.reshape(1, D), w["w_in"], w["g_q"].reshape(1, q_rank), w["g_kv"].reshape(1, kv_rank),
      w["w_qn"], w["w_qr"], w["w_uk"], tab)


def _prep_mla(w_mla_in, g_q, g_kv, w_uq, w_uk, w_uv, w_out):
    q_rank, heads, qd = w_uq.shape
    kv_rank, _, nope = w_uk.shape
    vdim = w_uv.shape[2]
    rope = qd - nope
    half = rope // 2
    assert kv_rank == MXU_TILE and MXU_TILE % nope == 0 and MXU_TILE % vdim == 0 and 2 * rope <= LANES
    d = w_mla_in.shape[0]
    wr = w_mla_in[:, q_rank + kv_rank:]
    w_in = jnp.concatenate(
        [w_mla_in, wr[:, half:], wr[:, :half], jnp.zeros((d, LANES - 2 * rope), F32)], axis=1).astype(BF16)
    w_qn = w_uq[:, :, :nope].reshape(q_rank, heads * nope).astype(BF16)
    x1, x2 = w_uq[:, :, nope:nope + half], w_uq[:, :, nope + half:]
    w_qr = jnp.concatenate([x1, x2, x2, x1, jnp.zeros((q_rank, heads, LANES - 2 * rope), F32)], axis=2)
    w_qr = jnp.transpose(w_qr, (1, 0, 2)).astype(BF16)
    per = MXU_TILE // nope
    slot = jax.nn.one_hot(jnp.arange(heads) % per, per, dtype=F32)
    w_ukp = jnp.einsum("rhd,hs->hsdr", w_uk, slot).reshape(heads, MXU_TILE, kv_rank).astype(BF16)
    perv = MXU_TILE // vdim
    slotv = jax.nn.one_hot(jnp.arange(heads) % perv, perv, dtype=F32)
    w_uvp = jnp.einsum("rhv,hs->hrsv", w_uv, slotv).reshape(heads, kv_rank, MXU_TILE).astype(BF16)
    return dict(heads=heads, q_rank=q_rank, kv_rank=kv_rank, rope=rope, nope=nope, vdim=vdim,
                scale=float((nope + rope) ** -0.5 * math.log2(math.e)), w_in=w_in, g_q=g_q, g_kv=g_kv, w_qn=w_qn,
                w_qr=w_qr, w_uk=w_ukp, w_uv=w_uvp, w_out=w_out.astype(BF16))


def _rope_table(pos, rope):
    half = rope // 2
    inv = ROPE_BASE ** (-jnp.arange(half, dtype=F32) / half)
    ang = pos.astype(F32)[:, None] * inv[None, :]
    c, s = jnp.cos(ang), jnp.sin(ang)
    return jnp.concatenate([c, c, -s, s, jnp.zeros((pos.shape[0], LANES - 2 * rope), F32)], axis=1)


def _lane_tile(stat, width):
    if width % LANES:
        return stat[:, :1]
    return stat if width == LANES else jnp.tile(stat, (1, width // LANES))


def _softmax_step(s, v, m_scr, l_scr, acc_scr):
    m_prev = m_scr[...]
    m_new = jnp.maximum(m_prev, jnp.max(s, axis=1, keepdims=True))
    alpha = jnp.exp2(m_prev - m_new)
    p = jnp.exp2(s - _lane_tile(m_new, s.shape[1]))
    l_scr[...] = alpha * l_scr[...] + jnp.sum(p, axis=1, keepdims=True)
    acc_scr[...] = acc_scr[...] * _lane_tile(alpha, v.shape[1]) + _dot(p.astype(BF16), v)
    m_scr[...] = m_new


def _softmax_result(l_scr, acc_scr):
    return acc_scr[...] / _lane_tile(l_scr[...], acc_scr.shape[-1])


def _softmax_init(m_scr, l_scr, acc_scr):
    m_scr[...] = jnp.full_like(m_scr, -jnp.inf)
    l_scr[...] = jnp.zeros_like(l_scr)
    acc_scr[...] = jnp.zeros_like(acc_scr)


def _attn_prompt_body(qi_ref, kj_ref, q_ref, k_ref, x_ref, wuv_ref, wo_ref, y_ref, m_scr, l_scr, acc_scr,
                      *, tq, tk, kv_rank, ahead):
    s_id = pl.program_id(1)
    qi, kj = qi_ref[s_id], kj_ref[s_id]
    heads = q_ref.shape[1]
    groups, rows = m_scr.shape[0], m_scr.shape[1]
    hb = heads // groups

    @pl.when(kj == 0)
    def _():
        _softmax_init(m_scr, l_scr, acc_scr)

    k = k_ref[0]
    v = k[:, :kv_rank]
    qpos = qi * tq + lax.broadcasted_iota(jnp.int32, (tq, tk), 0)
    kpos = kj * tk + lax.broadcasted_iota(jnp.int32, (tq, tk), 1)
    bias = jnp.where(kpos <= qpos, 0.0, -jnp.inf)

    def scores(g):
        q = q_ref[0, g * hb:(g + 1) * hb].reshape(rows, -1)
        return (_dot_nt(q, k).reshape(hb, tq, tk) + bias[None]).reshape(rows, tk)

    pending = [scores(g) for g in range(min(ahead, groups))]
    for g in range(groups):
        if g + ahead < groups:
            pending.append(scores(g + ahead))
        _softmax_step(pending.pop(0), v, m_scr.at[g], l_scr.at[g], acc_scr.at[g])

    @pl.when(kj == ((qi + 1) * tq - 1) // tk)
    def _():
        def latent_out(hd):
            g, r = divmod(hd, hb)
            o = _softmax_result(l_scr.at[g], acc_scr.at[g])[r * tq:(r + 1) * tq]
            return o.astype(BF16)

        y_ref[0] = _mla_project_out(latent_out, heads, x_ref[0], wuv_ref, wo_ref)


def _mla_project_out(latent_out, heads, x, wuv_ref, wo_ref):
    groups = wo_ref.shape[0] // MXU_TILE
    per = heads // groups
    vs = []
    for g in range(groups):
        acc = None
        for hd in range(g * per, (g + 1) * per):
            part = _dot(latent_out(hd), wuv_ref[hd])
            acc = part if acc is None else acc + part
        vs.append(acc.astype(BF16))
    v = jnp.concatenate(vs, axis=1)
    return x + _dot(v, wo_ref[...])


def _attn_prompt(q, k, x, w, tq=512, tk=512, hb=1, ahead=1):
    kv_rank = w["kv_rank"]
    D = x.shape[-1]
    resident = pl.Buffered(1)
    B, heads, T, kw = q.shape
    tq, tk = min(tq, T), min(tk, T)
    groups, rows = heads // hb, hb * tq
    pairs = [(i, j) for i in range(T // tq) for j in range(((i + 1) * tq - 1) // tk + 1)]
    qi = jnp.asarray(np.array([p[0] for p in pairs], np.int32))
    kj = jnp.asarray(np.array([p[1] for p in pairs], np.int32))
    grid_spec = pltpu.PrefetchScalarGridSpec(
        num_scalar_prefetch=2,
        grid=(B, len(pairs)),
        in_specs=[
            pl.BlockSpec((1, heads, tq, kw), lambda b, s, qi, kj: (b, 0, qi[s], 0)),
            pl.BlockSpec((1, tk, kw), lambda b, s, qi, kj: (b, kj[s], 0)),
            pl.BlockSpec((1, tq, D), lambda b, s, qi, kj: (b, qi[s], 0)),
            pl.BlockSpec(w["w_uv"].shape, lambda b, s, qi, kj: (0, 0, 0), pipeline_mode=resident),
            pl.BlockSpec(w["w_out"].shape, lambda b, s, qi, kj: (0, 0), pipeline_mode=resident),
        ],
        out_specs=pl.BlockSpec((1, tq, D), lambda b, s, qi, kj: (b, qi[s], 0)),
        scratch_shapes=[pltpu.VMEM((groups, rows, LANES), F32), pltpu.VMEM((groups, rows, LANES), F32),
                        pltpu.VMEM((groups, rows, kv_rank), F32)],
    )
    return pl.pallas_call(
        functools.partial(_attn_prompt_body, tq=tq, tk=tk, kv_rank=kv_rank, ahead=ahead),
        out_shape=jax.ShapeDtypeStruct(x.shape, F32),
        grid_spec=grid_spec,
        compiler_params=_cparams("parallel", "arbitrary"),
        name="attn_prompt",
    )(qi, kj, q, k, x, w["w_uv"], w["w_out"])


def _attn_decode_body(pt_ref, q_ref, kn_ref, ck_hbm, kr_hbm, o_ref, ck_buf, kr_buf, sem, m_scr, l_scr, acc_scr,
                      *, layer, kv_rank, rope):
    step, nsteps = pl.program_id(0), pl.num_programs(0)
    nb, heads, tn = q_ref.shape[0], q_ref.shape[1], q_ref.shape[2]
    pages, page = ck_buf.shape[2], ck_buf.shape[3]
    n_chunks = pt_ref.shape[1] // pages
    rows = heads * tn

    def page_copies(slot, page_id):
        copies = []
        for e in range(nb):
            for j in range(pages):
                pid = page_id(e, j)
                copies.append(pltpu.make_async_copy(ck_hbm.at[layer, pid], ck_buf.at[slot, e, j], sem.at[slot]))
                copies.append(pltpu.make_async_copy(kr_hbm.at[layer, pid], kr_buf.at[slot, e, j], sem.at[slot]))
        return copies

    def start_chunk(slot, at_step, chunk):
        copies = page_copies(slot, lambda e, j: pt_ref[at_step * nb + e, chunk * pages + j])
        for n, c in enumerate(copies):
            c.start(priority=(n // 2) % 2)

    def wait_chunk(slot):
        for c in page_copies(slot, lambda e, j: 0):
            c.wait()

    @pl.when(step == 0)
    def _():
        start_chunk(0, 0, 0)

    _softmax_init(m_scr, l_scr, acc_scr)

    def queries(e):
        q = q_ref[e].reshape(rows, -1)
        qy = q[:, kv_rank:].astype(F32)
        qr = (qy + pltpu.roll(qy, LANES - rope, 1))[:, :rope].astype(BF16)
        return q, q[:, :kv_rank], qr

    def attend(slot):
        scores = []
        for e in range(nb):
            _, qa, qr = queries(e)
            kc = ck_buf[slot, e].reshape(pages * page, kv_rank).astype(BF16)
            krt = jnp.concatenate([kr_buf[slot, e, j].astype(BF16) for j in range(pages)], axis=1)
            scores.append((_dot_nt(qa, kc) + _dot(qr, krt), kc))
        for e, (s, kc) in enumerate(scores):
            _softmax_step(s, kc, m_scr.at[e], l_scr.at[e], acc_scr.at[e])

    def chunk_pair(c2, carry):
        start_chunk(1, step, 2 * c2 + 1)
        wait_chunk(0)
        attend(0)
        last = c2 == n_chunks // 2 - 1
        nxt_step = jnp.where(last, step + 1, step)
        nxt_chunk = jnp.where(last, 0, 2 * c2 + 2)

        @pl.when(nxt_step < nsteps)
        def _():
            start_chunk(0, nxt_step, nxt_chunk)

        wait_chunk(1)
        attend(1)
        return carry

    lax.fori_loop(0, n_chunks // 2, chunk_pair, 0)

    for e in range(nb):
        q, _, _ = queries(e)
        kn = kn_ref[e]
        s = _dot_nt(q, kn)
        tq_ = lax.broadcasted_iota(jnp.int32, (heads, tn, tn), 1).reshape(rows, tn)
        tk_ = lax.broadcasted_iota(jnp.int32, (rows, tn), 1)
        s = jnp.where(tk_ <= tq_, s, -jnp.inf)
        _softmax_step(s, kn[:, :kv_rank], m_scr.at[e], l_scr.at[e], acc_scr.at[e])
        o = _softmax_result(l_scr.at[e], acc_scr.at[e])
        o_ref[e] = o.astype(BF16).reshape(heads, tn, kv_rank)


def _attn_decode(q, k_new, cache_ckv, cache_krope_t, layer, page_table, kv_rank, rope, pages=16, nb=4):
    B, heads, tn, kw = q.shape
    n_pages = page_table.shape[1]
    page = cache_ckv.shape[2]
    nb = math.gcd(nb, B)
    pages = math.gcd(pages, n_pages // 2)
    assert (n_pages // pages) % 2 == 0, "the two-slot ring walks chunks in pairs"
    rows = heads * tn
    grid_spec = pltpu.PrefetchScalarGridSpec(
        num_scalar_prefetch=1,
        grid=(B // nb,),
        in_specs=[pl.BlockSpec((nb, heads, tn, kw), lambda b, pt: (b, 0, 0, 0)),
                  pl.BlockSpec((nb, tn, kw), lambda b, pt: (b, 0, 0)),
                  pl.BlockSpec(memory_space=pl.ANY),
                  pl.BlockSpec(memory_space=pl.ANY)],
        out_specs=pl.BlockSpec((nb, heads, tn, kv_rank), lambda b, pt: (b, 0, 0, 0)),
        scratch_shapes=[pltpu.VMEM((2, nb, pages, page, kv_rank), cache_ckv.dtype),
                        pltpu.VMEM((2, nb, pages, rope, page), cache_krope_t.dtype),
                        pltpu.SemaphoreType.DMA((2,)),
                        pltpu.VMEM((nb, rows, LANES), F32), pltpu.VMEM((nb, rows, LANES), F32),
                        pltpu.VMEM((nb, rows, kv_rank), F32)],
    )
    return pl.pallas_call(
        functools.partial(_attn_decode_body, layer=layer, kv_rank=kv_rank, rope=rope),
        out_shape=jax.ShapeDtypeStruct((B, heads, tn, kv_rank), BF16),
        grid_spec=grid_spec,
        compiler_params=_cparams("arbitrary"),
        name="attn_decode",
    )(page_table, q, k_new, cache_ckv, cache_krope_t)


def _mla_out_body(o_ref, x_ref, wuv_ref, wo_ref, y_ref):
    bb, heads, tt, kv_rank = o_ref.shape
    rows = bb * tt
    x = x_ref[...].reshape(rows, -1)
    y = _mla_project_out(lambda hd: o_ref[:, hd].reshape(rows, kv_rank), heads, x, wuv_ref, wo_ref)
    y_ref[...] = y.reshape(y_ref.shape)


def _mla_out(o, x, w, rows=256):
    B, T, D = x.shape
    heads, kv_rank = o.shape[1], o.shape[3]
    bb, tt = _token_tiles(B, T, rows)
    return pl.pallas_call(
        _mla_out_body,
        out_shape=jax.ShapeDtypeStruct(x.shape, F32),
        grid=(B // bb, T // tt),
        in_specs=[pl.BlockSpec((bb, heads, tt, kv_rank), lambda b, t: (b, 0, t, 0)),
                  _tok_spec(bb, tt, D), _full(w["w_uv"].shape), _full(w["w_out"].shape)],
        out_specs=_tok_spec(bb, tt, D),
        compiler_params=_cparams("parallel", "parallel"),
        name="mla_out",
    )(o, x, w["w_uv"], w["w_out"])


def _mlstm_proj_body(x_ref, g_ref, w_ref, q_ref, k_ref, v_ref, o_ref, gt_ref, *, qk, hv, qscale):
    rows = x_ref.shape[0] * x_ref.shape[1]
    x = x_ref[...].reshape(rows, -1)
    h = _rmsnorm(x, g_ref[...]).astype(BF16)
    z = _dot(h, w_ref[...])
    q_ref[...] = (z[:, :qk] * qscale).astype(BF16).reshape(q_ref.shape)
    k_ref[...] = z[:, qk:2 * qk].astype(BF16).reshape(k_ref.shape)
    v_ref[...] = z[:, 2 * qk:2 * qk + hv].astype(BF16).reshape(v_ref.shape)
    o_ref[...] = z[:, 2 * qk + hv:2 * qk + 2 * hv].reshape(o_ref.shape)
    gt_ref[...] = z[:, 2 * qk + 2 * hv:].reshape(gt_ref.shape)


def _mlstm_proj(x, g, w, qk, hv, dk, rows=512):
    B, T, D = x.shape
    bb, tt = _token_tiles(B, T, rows)
    widths = (qk, qk, hv, hv, LANES)
    dtypes = (BF16, BF16, BF16, F32, F32)
    return pl.pallas_call(
        functools.partial(_mlstm_proj_body, qk=qk, hv=hv, qscale=float(dk ** -0.5)),
        out_shape=tuple(jax.ShapeDtypeStruct((B, T, wd), dt) for wd, dt in zip(widths, dtypes)),
        grid=(B // bb, T // tt),
        in_specs=[_tok_spec(bb, tt, D), _full((1, D)), _full(w.shape)],
        out_specs=tuple(_tok_spec(bb, tt, wd) for wd in widths),
        compiler_params=_cparams("parallel", "parallel"),
        name="mlstm_proj",
    )(x, g.reshape(1, D), w)


def _mlstm_body(q_ref, k_ref, v_ref, o_ref, gt_ref, x_ref, c0_ref, n0_ref, m0_ref, bias_ref, gh_ref, wo_ref,
                y_ref, c_ref, n_ref, m_ref, c_scr, n_scr, m_scr, *, heads, dk, dv):
    ci = pl.program_id(1)
    nb, L = q_ref.shape[0], q_ref.shape[1]

    @pl.when(ci == 0)
    def _():
        c_scr[...] = c0_ref[...]
        n_scr[...] = n0_ref[...]
        m_scr[...] = m0_ref[...]

    lane = lax.broadcasted_iota(jnp.int32, (L, LANES), 1)
    row = lax.broadcasted_iota(jnp.int32, (L, L), 0)
    col = lax.broadcasted_iota(jnp.int32, (L, L), 1)
    causal = col <= row
    tri = causal.astype(F32)

    gates, pre = [], {}
    for e in range(nb):
        a = GATE_CAP * jnp.tanh((gt_ref[e] + bias_ref[...]) / GATE_CAP)
        lf = -(jnp.maximum(-a, 0.0) + jnp.log1p(jnp.exp(-jnp.abs(a))))
        xg = jnp.where(lane < heads, a, jnp.where(lane < 2 * heads, lf, 0.0))
        bc = jnp.dot(tri, xg, precision=HIGHEST, preferred_element_type=F32)
        gates.append((xg, bc, xg.T, bc.T))
    for e in range(nb):
        for hd in range(heads):
            qh = q_ref[e, :, hd * dk:(hd + 1) * dk]
            kh = k_ref[e, :, hd * dk:(hd + 1) * dk]
            pre[e, hd] = (_dot_nt(qh, kh), _dot(qh, c_scr[e, hd].astype(BF16)))

    for e in range(nb):
        xg, bc, xg_t, bc_t = gates[e]
        blast_row = bc[L - 1:L, :]
        outs, post = [], []
        for hd in range(heads):
            qk_, qc = pre[e, hd]
            b_col = jnp.sum(jnp.where(lane == heads + hd, bc, 0.0), axis=1, keepdims=True)
            ib_row = xg_t[hd:hd + 1, :] - bc_t[heads + hd:heads + hd + 1, :]
            log_d = jnp.where(causal, b_col + ib_row, -jnp.inf)
            i_col = jnp.sum(jnp.where(lane == hd, xg, 0.0), axis=1, keepdims=True)
            b_last = jnp.sum(jnp.where(lane[:1] == heads + hd, blast_row, 0.0), axis=1, keepdims=True)
            m_prev = m_scr[e, hd:hd + 1, :1]
            m_t = jnp.maximum(b_col + m_prev, jnp.max(log_d, axis=1, keepdims=True))
            decay = jnp.exp(b_col + m_prev - m_t)
            qh = q_ref[e, :, hd * dk:(hd + 1) * dk]
            kh = k_ref[e, :, hd * dk:(hd + 1) * dk]
            vh = v_ref[e, :, hd * dv:(hd + 1) * dv]
            c_prev = c_scr[e, hd]
            n_prev = n_scr[e, hd:hd + 1, :]
            sm = qk_ * jnp.exp(log_d - m_t)
            num = _dot(sm.astype(BF16), vh) + decay * qc
            den = (jnp.sum(sm, axis=1, keepdims=True)
                   + decay * jnp.sum(qh.astype(F32) * n_prev, axis=1, keepdims=True))
            hh = num / jnp.maximum(jnp.abs(den), jnp.exp(-m_t))
            hh = (hh * lax.rsqrt(jnp.mean(hh * hh, axis=1, keepdims=True) + NORM_EPS)
                  * gh_ref[:, hd * dv:(hd + 1) * dv])
            outs.append((jax.nn.sigmoid(o_ref[e, :, hd * dv:(hd + 1) * dv]) * hh).astype(BF16))
            post.append((hd, m_t, m_prev, i_col, b_col, b_last, kh, vh, c_prev, n_prev))
        for hd, m_t, m_prev, i_col, b_col, b_last, kh, vh, c_prev, n_prev in post:
            m_new = m_t[L - 1:L, :]
            wgt = jnp.exp(i_col + b_last - b_col - m_new)
            cdec = jnp.exp(b_last + m_prev - m_new)
            kw_ = kh.astype(F32) * wgt
            c_scr[e, hd] = cdec * c_prev + _dot_tn(kw_.astype(BF16), vh)
            n_scr[e, hd:hd + 1, :] = cdec * n_prev + jnp.sum(kw_, axis=0, keepdims=True)
            m_scr[e, hd:hd + 1, :] = jnp.broadcast_to(m_new, (1, LANES))
        out = jnp.concatenate(outs, axis=1)
        y_ref[e] = x_ref[e] + _dot(out, wo_ref[...])

    @pl.when(ci == pl.num_programs(1) - 1)
    def _():
        c_ref[...] = c_scr[...]
        n_ref[...] = n_scr[...]
        m_ref[...] = m_scr[...]


def _mlstm(q, k, v, o, gt, x, c0, n0, m0, bias, g_head, w_out, chunk=256, nb=1):
    B, T, D = x.shape
    heads, dk, dv = c0.shape[1], c0.shape[2], c0.shape[3]
    L = min(chunk, T)
    nb = math.gcd(nb, B)
    tok = lambda wd: pl.BlockSpec((nb, L, wd), lambda b, c: (b, c, 0))
    st3 = pl.BlockSpec((nb, heads, LANES), lambda b, c: (b, 0, 0))
    st4 = pl.BlockSpec((nb, heads, dk, dv), lambda b, c: (b, 0, 0, 0))
    return pl.pallas_call(
        functools.partial(_mlstm_body, heads=heads, dk=dk, dv=dv),
        out_shape=(jax.ShapeDtypeStruct(x.shape, F32),
                   jax.ShapeDtypeStruct(c0.shape, F32),
                   jax.ShapeDtypeStruct((B, heads, dk), F32),
                   jax.ShapeDtypeStruct((B, heads, LANES), F32)),
        grid=(B // nb, T // L),
        in_specs=[tok(heads * dk), tok(heads * dk), tok(heads * dv), tok(heads * dv), tok(LANES), tok(D),
                  st4, st3, st3, _full((1, LANES)), _full((1, heads * dv)), _full(w_out.shape)],
        out_specs=(tok(D), st4, st3, st3),
        scratch_shapes=[pltpu.VMEM((nb, heads, dk, dv), F32), pltpu.VMEM((nb, heads, dk), F32),
                        pltpu.VMEM((nb, heads, LANES), F32)],
        compiler_params=_cparams("parallel", "arbitrary"),
        name="mlstm",
    )(q, k, v, o, gt, x, c0, n0, m0, bias, g_head.reshape(1, heads * dv), w_out)


def kernel(x_prompt, x_sample, cache_ckv, cache_krope, state_C, state_n, state_m, page_table, p_prompt, p_sample, g_ffa, w_ffa_in, w_ffa_out, g_mix, g_ffb, w_ffb_in, w_ffb_out, g_ple, w_ple_gate, w_ple_proj, w_mla_in, g_q_lat, g_kv_lat, w_uq, w_uk, w_uv, w_mla_out, w_mlstm_in, b_igate, b_fgate, g_mlstm_head, w_mlstm_out, g_final):
    depth = g_ffa.shape[0]
    seq, dec_seq = x_prompt.shape[1], x_sample.shape[1]
    past = page_table.shape[1] * cache_ckv.shape[2]
    positions = (jnp.arange(seq), past + jnp.arange(dec_seq))
    xs = [x_prompt, x_sample]
    ps = (p_prompt, p_sample)
    heads_m, dk, dv = state_C.shape[2], state_C.shape[3], state_C.shape[4]
    assert dk == LANES and 2 * heads_m <= LANES
    qk, hv = heads_m * dk, heads_m * dv
    outs = {name: ([], []) for name in ("ckv", "kr", "C", "n", "m")}
    krope_t = jnp.swapaxes(cache_krope, 2, 3)

    for i in range(depth):
        j = i // 2
        bf = lambda a: a.astype(BF16)
        w_ffa = (bf(w_ffa_in[i]), bf(w_ffa_out[i]))
        w_ffb = (bf(w_ffb_in[i]), bf(w_ffb_out[i]))
        w_gate, w_proj = bf(w_ple_gate[i]), bf(w_ple_proj[i])
        if i % 2 == 0:
            mla = _prep_mla(w_mla_in[j], g_q_lat[j], g_kv_lat[j], w_uq[j], w_uk[j], w_uv[j], w_mla_out[j])
        else:
            wi = w_mlstm_in[j]
            pad = LANES - 2 * heads_m
            w_ml = bf(jnp.pad(wi, ((0, 0), (0, pad))))
            bias = jnp.pad(jnp.concatenate([b_igate[j], b_fgate[j]]), (0, pad)).reshape(1, LANES)
            w_mo = bf(w_mlstm_out[j])
        for g in range(2):
            x = xs[g]
            nb = x.shape[0]
            x = _ffn(x, g_ffa[i], *w_ffa)
            if i % 2 == 0:
                tab = _rope_table(positions[g], mla["rope"])
                q, k, ckv, kr = _mla_proj(x, g_mix[i], mla, tab)
                if g == 0:
                    x = _attn_prompt(q, k, x, mla)
                else:
                    o = _attn_decode(q, k, cache_ckv, krope_t, j, page_table, mla["kv_rank"], mla["rope"])
                    x = _mla_out(o, x, mla)
                outs["ckv"][g].append(ckv)
                outs["kr"][g].append(kr)
            else:
                if g == 0:
                    c0 = jnp.zeros((nb, heads_m, dk, dv), F32)
                    n0 = jnp.zeros((nb, heads_m, dk), F32)
                    m0 = jnp.zeros((nb, heads_m), F32)
                else:
                    c0, n0, m0 = state_C[j], state_n[j], state_m[j]
                m0 = jnp.broadcast_to(m0[:, :, None], (nb, heads_m, LANES))
                qm, km, vm, om, gt = _mlstm_proj(x, g_mix[i], w_ml, qk, hv, dk)
                x, c_new, n_new, m_new = _mlstm(qm, km, vm, om, gt, x, c0, n0, m0, bias, g_mlstm_head[j], w_mo,
                                                nb=1 if g == 0 else 4)
                outs["C"][g].append(c_new)
                outs["n"][g].append(n_new)
                outs["m"][g].append(m_new[:, :, 0])
            x = _ffn(x, g_ffb[i], *w_ffb, ple=(ps[g][i], g_ple[i], w_gate, w_proj, g_final, i == depth - 1))
            xs[g] = x

    st = lambda name, g: jnp.stack(outs[name][g])
    return (xs[0], xs[1], st("ckv", 0), st("kr", 0), st("ckv", 1), st("kr", 1),
            st("C", 0), st("n", 0), st("m", 0), st("C", 1), st("n", 1), st("m", 1))
```
